```python
import math
import jax, jax.numpy as jnp
from jax import lax
import numpy as np

D_MODEL = 1024
BATCH = 8
SEQ = 4096
DEPTH = 2

HEAD_DIM = 64
D_MIX = D_MODEL
DN_HEADS = 6
DN_WIDTH = DN_HEADS * HEAD_DIM
FX_HEADS = 6
FX_WIDTH = FX_HEADS * HEAD_DIM
POOL_GROUPS = 4
POOL_WIDTH = D_MIX - DN_WIDTH - FX_WIDTH
POOL_GROUP_DIM = POOL_WIDTH // POOL_GROUPS
POOL_WINDOWS = (2, 4, 8, 16)
CONV_WIDTH = 4
DN_CHUNK = 64
FX_BLOCK = 128
D_FF = 2816
N_EXPERTS = 8
TOP_K = 2
D_FF_EXPERT = 1536
N_DENSE = (DEPTH + 1) // 2
N_MOE = DEPTH // 2
EPS = 1e-6
N_IN = 4 * DN_WIDTH + 2 * DN_HEADS + 3 * FX_WIDTH + FX_HEADS + POOL_WIDTH

kernel_name = 'hybrid_deltanet_fox_pool_moe'

F32 = jnp.float32


def rmsnorm(x, g):
    xf = x.astype(F32)
    y = xf * lax.rsqrt(jnp.mean(xf * xf, axis=-1, keepdims=True) + EPS)
    return (y * g.astype(F32)).astype(x.dtype)


def l2norm(x):
    return x * lax.rsqrt(jnp.sum(x * x, axis=-1, keepdims=True) + EPS)


def causal_conv(x, w):
    c = x.shape[-1]
    return lax.conv_general_dilated(x, w.astype(x.dtype)[:, None, :], window_strides=(1,),
                                    padding=[(CONV_WIDTH - 1, 0)],
                                    dimension_numbers=('NWC', 'WIO', 'NWC'),
                                    feature_group_count=c)


def chunk_gated_delta_rule(q, k, v, g, beta):
    B, T, H, Dk = q.shape
    Dv = v.shape[-1]
    C = DN_CHUNK
    N = T // C

    def to_chunks(a):
        a = a.reshape((B, N, C, H) + a.shape[3:])
        return jnp.moveaxis(a, 3, 1)

    q, k, v, g, beta = (to_chunks(a) for a in (q, k, v, g, beta))
    G = jnp.cumsum(g, axis=-1)
    causal = jnp.tril(jnp.ones((C, C), bool))
    strict = jnp.tril(jnp.ones((C, C), bool), -1)
    diff = G[..., :, None] - G[..., None, :]
    decay = jnp.where(causal, jnp.exp(jnp.where(causal, diff, 0.0)), 0.0)
    kb = k * beta[..., None]
    a_mat = jnp.where(strict, jnp.einsum('bhnid,bhnjd->bhnij', kb, k) * decay, 0.0)
    eye = jnp.eye(C, dtype=q.dtype)
    rhs = jnp.concatenate([v * beta[..., None], kb * jnp.exp(G)[..., None]], axis=-1)
    uw = lax.linalg.triangular_solve(a_mat + eye, rhs, left_side=True, lower=True,
                                     transpose_a=False, conjugate_a=False, unit_diagonal=True)
    u, w = uw[..., :Dv], uw[..., Dv:]
    qk = jnp.where(causal, jnp.einsum('bhnid,bhnjd->bhnij', q, k) * decay, 0.0)
    q_dec = q * jnp.exp(G)[..., None]
    k_dec = k * jnp.exp(G[..., -1:] - G)[..., None]
    last = jnp.exp(G[..., -1])

    def step(S, xs):
        u_n, w_n, qk_n, q_n, k_n, l_n = xs
        v_new = u_n - jnp.einsum('bhck,bhkv->bhcv', w_n, S)
        o_n = jnp.einsum('bhck,bhkv->bhcv', q_n, S) + jnp.einsum('bhij,bhjv->bhiv', qk_n, v_new)
        S = S * l_n[..., None, None] + jnp.einsum('bhck,bhcv->bhkv', k_n, v_new)
        return S, o_n

    xs = tuple(jnp.moveaxis(a, 2, 0) for a in (u, w, qk, q_dec, k_dec, last))
    S0 = jnp.zeros((B, H, Dk, Dv), q.dtype)
    _, o = lax.scan(step, S0, xs)
    return jnp.transpose(o, (1, 0, 3, 2, 4)).reshape(B, T, H, Dv)


def gated_deltanet(q, k, v, z, a, b, conv_w, a_log, dt_bias, onorm_g):
    B, T, _ = q.shape
    dt = q.dtype
    qkv = jax.nn.silu(causal_conv(jnp.concatenate([q, k, v], axis=-1), conv_w)).astype(F32)
    q, k, v = jnp.split(qkv, 3, axis=-1)
    heads = lambda t: t.reshape(B, T, DN_HEADS, HEAD_DIM)
    q = l2norm(heads(q)) * (HEAD_DIM ** -0.5)
    k = l2norm(heads(k))
    v = heads(v)
    beta = jax.nn.sigmoid(b.astype(F32))
    g = -jnp.exp(a_log.astype(F32)) * jax.nn.softplus(a.astype(F32) + dt_bias.astype(F32))
    o = chunk_gated_delta_rule(q, k, v, g, beta)
    o = rmsnorm(o, onorm_g) * jax.nn.silu(heads(z).astype(F32))
    return o.reshape(B, T, DN_WIDTH).astype(dt)


def forgetting_attention(q, k, v, f, qnorm_g, knorm_g, f_bias):
    B, T, _ = q.shape
    dt = q.dtype
    heads = lambda t: t.reshape(B, T, FX_HEADS, HEAD_DIM).transpose(0, 2, 1, 3)
    q = rmsnorm(heads(q), qnorm_g)
    k = rmsnorm(heads(k), knorm_g)
    v = heads(v)
    logf = jax.nn.log_sigmoid(f.astype(F32) + f_bias.astype(F32))
    c = jnp.cumsum(logf, axis=1).transpose(0, 2, 1)
    scale = HEAD_DIM ** -0.5
    outs = []
    for i in range(T // FX_BLOCK):
        lo, hi = i * FX_BLOCK, (i + 1) * FX_BLOCK
        s = jnp.einsum('bhqd,bhkd->bhqk', q[:, :, lo:hi], k[:, :, :hi]).astype(F32) * scale
        s = s + c[:, :, lo:hi, None] - c[:, :, None, :hi]
        q_pos = jnp.arange(lo, hi)[:, None]
        k_pos = jnp.arange(hi)[None, :]
        s = jnp.where(k_pos <= q_pos, s, -jnp.inf)
        p = jax.nn.softmax(s, axis=-1).astype(dt)
        outs.append(jnp.einsum('bhqk,bhkd->bhqd', p, v[:, :, :hi]))
    o = jnp.concatenate(outs, axis=2)
    return o.transpose(0, 2, 1, 3).reshape(B, T, FX_WIDTH)


def pool_mixer(xp, pool_w, pool_scale):
    B, T, _ = xp.shape
    xg = xp.reshape(B, T, POOL_GROUPS, POOL_GROUP_DIM).astype(F32)
    cs = jnp.concatenate([jnp.zeros((B, 1, POOL_GROUPS, POOL_GROUP_DIM), F32),
                          jnp.cumsum(xg, axis=1)], axis=1)
    win = jnp.array(POOL_WINDOWS, jnp.int32)
    t1 = jnp.arange(1, T + 1, dtype=jnp.int32)[:, None]
    start = jnp.maximum(t1 - win[None, :], 0)
    count = (t1 - start).astype(F32)
    gidx = jnp.arange(POOL_GROUPS)[None, :]
    window_sum = cs[:, 1:] - cs[:, start, gidx]
    y = (window_sum / count[..., None] - xg).astype(xp.dtype)
    y = jnp.einsum('btgc,gcd->btgd', y, pool_w)
    return y.reshape(B, T, POOL_WIDTH) * pool_scale


def hybrid_mixer(h, w_in, dn_conv, dn_a_log, dn_dt_bias, dn_onorm, fx_qnorm, fx_knorm, fx_f_bias,
                 pool_w, pool_scale, w_out):
    proj = h @ w_in
    sizes = (DN_WIDTH, DN_WIDTH, DN_WIDTH, DN_WIDTH, DN_HEADS, DN_HEADS,
             FX_WIDTH, FX_WIDTH, FX_WIDTH, FX_HEADS, POOL_WIDTH)
    (dn_q, dn_k, dn_v, dn_z, dn_a, dn_b, fx_q, fx_k, fx_v, fx_f, pl_x) = jnp.split(
        proj, np.cumsum(sizes)[:-1].tolist(), axis=-1)
    o_dn = gated_deltanet(dn_q, dn_k, dn_v, dn_z, dn_a, dn_b, dn_conv, dn_a_log, dn_dt_bias, dn_onorm)
    o_fx = forgetting_attention(fx_q, fx_k, fx_v, fx_f, fx_qnorm, fx_knorm, fx_f_bias)
    o_pl = pool_mixer(pl_x, pool_w, pool_scale)
    return jnp.concatenate([o_dn, o_fx, o_pl], axis=-1) @ w_out


def swiglu(h, w_gate, w_up, w_down):
    return (jax.nn.silu(h @ w_gate) * (h @ w_up)) @ w_down


def moe_swiglu(h, router, w_gate, w_up, w_down):
    B, T, D = h.shape
    ht = h.reshape(B * T, D)
    probs = jax.nn.softmax((ht @ router).astype(F32), axis=-1)
    top_p, top_i = lax.top_k(probs, TOP_K)
    top_p = top_p / jnp.sum(top_p, axis=-1, keepdims=True)
    gates = jnp.sum(jax.nn.one_hot(top_i, N_EXPERTS, dtype=F32) * top_p[..., None], axis=1).astype(h.dtype)
    out = jnp.zeros_like(ht)
    for e in range(N_EXPERTS):
        out = out + gates[:, e:e + 1] * swiglu(ht, w_gate[e], w_up[e], w_down[e])
    return out.reshape(B, T, D)


def setup_inputs(seed: int = 0) -> dict:
    key = jax.random.key(seed)
    ks = list(jax.random.split(key, 32))
    counter = [0]

    def nk():
        counter[0] += 1
        return ks[counter[0] - 1]

    def nrm(shape, scale):
        return scale * jax.random.normal(nk(), shape, F32)

    def gain(shape):
        return 1.0 + nrm(shape, 0.05)

    L = DEPTH
    x = nrm((BATCH, SEQ, D_MODEL), 1.0)
    norm1 = gain((L, D_MODEL))
    w_in = nrm((L, D_MODEL, N_IN), D_MODEL ** -0.5)
    dn_conv = nrm((L, CONV_WIDTH, 3 * DN_WIDTH), CONV_WIDTH ** -0.5)
    dn_a_log = jnp.log(jax.random.uniform(nk(), (L, DN_HEADS), F32, 1.0, 16.0))
    dt0 = jnp.exp(jax.random.uniform(nk(), (L, DN_HEADS), F32, math.log(1e-3), math.log(1e-1)))
    dn_dt_bias = dt0 + jnp.log(-jnp.expm1(-dt0))
    dn_onorm = gain((L, HEAD_DIM))
    fx_qnorm = gain((L, HEAD_DIM))
    fx_knorm = gain((L, HEAD_DIM))
    fx_f_bias = 2.0 + nrm((L, FX_HEADS), 0.5)
    pool_w = nrm((L, POOL_GROUPS, POOL_GROUP_DIM, POOL_GROUP_DIM), POOL_GROUP_DIM ** -0.5)
    pool_scale = 1.0 + nrm((L, POOL_WIDTH), 0.1)
    w_out = nrm((L, D_MIX, D_MODEL), 0.5 * D_MIX ** -0.5)
    norm2 = gain((L, D_MODEL))
    ffn_gate = nrm((N_DENSE, D_MODEL, D_FF), D_MODEL ** -0.5)
    ffn_up = nrm((N_DENSE, D_MODEL, D_FF), D_MODEL ** -0.5)
    ffn_down = nrm((N_DENSE, D_FF, D_MODEL), 0.5 * D_FF ** -0.5)
    router = nrm((N_MOE, D_MODEL, N_EXPERTS), D_MODEL ** -0.5)
    moe_gate = nrm((N_MOE, N_EXPERTS, D_MODEL, D_FF_EXPERT), D_MODEL ** -0.5)
    moe_up = nrm((N_MOE, N_EXPERTS, D_MODEL, D_FF_EXPERT), D_MODEL ** -0.5)
    moe_down = nrm((N_MOE, N_EXPERTS, D_FF_EXPERT, D_MODEL), 0.5 * D_FF_EXPERT ** -0.5)
    return {'x': x, 'norm1': norm1, 'w_in': w_in, 'dn_conv': dn_conv, 'dn_a_log': dn_a_log,
            'dn_dt_bias': dn_dt_bias, 'dn_onorm': dn_onorm, 'fx_qnorm': fx_qnorm, 'fx_knorm': fx_knorm,
            'fx_f_bias': fx_f_bias, 'pool_w': pool_w, 'pool_scale': pool_scale, 'w_out': w_out,
            'norm2': norm2, 'ffn_gate': ffn_gate, 'ffn_up': ffn_up, 'ffn_down': ffn_down,
            'router': router, 'moe_gate': moe_gate, 'moe_up': moe_up, 'moe_down': moe_down}


def reference(x, norm1, w_in, dn_conv, dn_a_log, dn_dt_bias, dn_onorm, fx_qnorm, fx_knorm, fx_f_bias,
              pool_w, pool_scale, w_out, norm2, ffn_gate, ffn_up, ffn_down, router, moe_gate, moe_up,
              moe_down):
    for layer in range(DEPTH):
        h = rmsnorm(x, norm1[layer])
        x = x + hybrid_mixer(h, w_in[layer], dn_conv[layer], dn_a_log[layer], dn_dt_bias[layer],
                             dn_onorm[layer], fx_qnorm[layer], fx_knorm[layer], fx_f_bias[layer],
                             pool_w[layer], pool_scale[layer], w_out[layer])
        h = rmsnorm(x, norm2[layer])
        j = layer // 2
        if layer % 2 == 0:
            x = x + swiglu(h, ffn_gate[j], ffn_up[j], ffn_down[j])
        else:
            x = x + moe_swiglu(h, router[j], moe_gate[j], moe_up[j], moe_down[j])
    return x
```

```python
import functools

import jax
import jax.numpy as jnp
import numpy as np
from jax import lax
from jax.experimental import pallas as pl
from jax.experimental.pallas import tpu as pltpu

F32 = jnp.float32
BF16 = jnp.bfloat16

D_MODEL = 1024
HEAD_DIM = 64
DN_HEADS = 6
DN_WIDTH = DN_HEADS * HEAD_DIM
FX_HEADS = 6
FX_WIDTH = FX_HEADS * HEAD_DIM
POOL_GROUPS = 4
POOL_GROUP_DIM = 64
POOL_WIDTH = POOL_GROUPS * POOL_GROUP_DIM
POOL_WINDOWS = (2, 4, 8, 16)
CONV_WIDTH = 4
DN_CHUNK = 64
D_FF = 2816
N_EXPERTS = 8
D_FF_EXPERT = 1536
EPS = 1e-6
LANES = 128
NEG_BIG = -1e30

QKV_W = 3 * DN_WIDTH
COL_DN_QKV = 0
COL_DN_Z = COL_DN_QKV + QKV_W
COL_FX_QKV = COL_DN_Z + DN_WIDTH
COL_POOL = COL_FX_QKV + QKV_W
N_BIG = COL_POOL + POOL_WIDTH
LANE_A = 0
LANE_B = DN_HEADS
LANE_F = 2 * DN_HEADS

VMEM_LIMIT = 56 * 1024 * 1024


def _mm(a, b):
    return jnp.dot(a.astype(BF16), b.astype(BF16), preferred_element_type=F32)


def _mm_nt(a, b):
    return lax.dot_general(a.astype(BF16), b.astype(BF16), (((1,), (1,)), ((), ())),
                           preferred_element_type=F32)


def _mm_tn(a, b):
    return lax.dot_general(a.astype(BF16), b.astype(BF16), (((0,), (0,)), ((), ())),
                           preferred_element_type=F32)


def _split_mm(a, b_bf16):
    hi = a.astype(BF16)
    lo = (a - hi.astype(F32)).astype(BF16)
    return (jnp.dot(hi, b_bf16, preferred_element_type=F32)
            + jnp.dot(lo, b_bf16, preferred_element_type=F32))


def _sigmoid(x):
    return 1.0 / (1.0 + jnp.exp(-x))


def _softplus(x):
    return jnp.maximum(x, 0.0) + jnp.log(1.0 + jnp.exp(-jnp.abs(x)))


def _cumsum_rows(x, period):
    ridx = lax.broadcasted_iota(jnp.int32, x.shape, 0) & (period - 1)
    s = 1
    while s < period:
        x = x + jnp.where(ridx >= s, pltpu.roll(x, s, axis=0), 0.0)
        s *= 2
    return x


def _inproj_kernel(x_ref, g_ref, w_ref, ws_ref, dnqkv_ref, dnz_ref, fxqkv_ref, pool_ref, small_ref):
    x = x_ref[...]
    ms = jnp.mean(x * x, axis=-1, keepdims=True)
    h = (x * lax.rsqrt(ms + EPS) * g_ref[...]).astype(BF16)

    def proj(lo, hi):
        return jnp.dot(h, w_ref[:, lo:hi], preferred_element_type=F32)

    dnqkv_ref[...] = proj(COL_DN_QKV, COL_DN_Z).astype(BF16)
    dnz_ref[...] = proj(COL_DN_Z, COL_FX_QKV).astype(BF16)
    fxqkv_ref[...] = proj(COL_FX_QKV, COL_POOL).astype(BF16)
    pool_ref[...] = proj(COL_POOL, N_BIG).astype(BF16)
    small_ref[...] = jnp.dot(h, ws_ref[...], preferred_element_type=F32)


def _inproj(xf, g, w_big, w_small, tm):
    n = xf.shape[0]
    out_shape = (
        jax.ShapeDtypeStruct((n, QKV_W), BF16),
        jax.ShapeDtypeStruct((n, DN_WIDTH), BF16),
        jax.ShapeDtypeStruct((n, QKV_W), BF16),
        jax.ShapeDtypeStruct((n, POOL_WIDTH), BF16),
        jax.ShapeDtypeStruct((n, LANES), F32),
    )
    row = lambda w: pl.BlockSpec((tm, w), lambda i: (i, 0))
    full = lambda a: pl.BlockSpec(a.shape, lambda i: (0,) * a.ndim)
    return pl.pallas_call(
        _inproj_kernel,
        grid=(n // tm,),
        in_specs=[row(D_MODEL), full(g), full(w_big), full(w_small)],
        out_specs=(row(QKV_W), row(DN_WIDTH), row(QKV_W), row(POOL_WIDTH), row(LANES)),
        out_shape=out_shape,
        compiler_params=pltpu.CompilerParams(dimension_semantics=("arbitrary",),
                                             vmem_limit_bytes=VMEM_LIMIT),
        name="inproj",
    )(xf, g, w_big, w_small)


def _unit_lower_inverse(a, r_i, c_i):
    def blk(shift):
        return (r_i >> shift) == (c_i >> shift)

    eye = (r_i == c_i).astype(F32)
    d = jnp.where(blk(3), a, 0.0)
    d2 = _mm(d, d)
    d4 = _mm(d2, d2)
    x = eye - d
    x = x + _mm(x, d2)
    x = x + _mm(x, d4)
    for shift in (3, 4, 5):
        e = jnp.where(jnp.logical_and(blk(shift + 1), jnp.logical_not(blk(shift))), a, 0.0)
        x = x - _mm(x, _mm(e, x))
    return x


def _dn_kernel(qkv_ref, z_ref, small_ref, convw_ref, gpar_ref, onorm_ref, bd_ref,
               o_ref, ext_ref, s_ref, oscr_ref, *, tc):
    i = pl.program_id(1)

    @pl.when(i == 0)
    def _():
        ext_ref[0:8, :] = jnp.zeros((8, QKV_W), F32)
        s_ref[...] = jnp.zeros_like(s_ref)

    x = qkv_ref[...].astype(F32)
    ext_ref[8:8 + tc, :] = x
    w = convw_ref[...]
    y = x * w[CONV_WIDTH - 1:CONV_WIDTH, :]
    for j in range(CONV_WIDTH - 1):
        y = y + ext_ref[pl.ds(8 - (CONV_WIDTH - 1) + j, tc), :] * w[j:j + 1, :]
    ext_ref[0:8, :] = x[tc - 8:tc, :]
    y = y * _sigmoid(y)

    bd = bd_ref[...]
    q = y[:, 0:DN_WIDTH]
    k = y[:, DN_WIDTH:2 * DN_WIDTH]
    v = y[:, 2 * DN_WIDTH:3 * DN_WIDTH]
    q = q * lax.rsqrt(_split_mm(q * q, bd) + EPS) * (HEAD_DIM ** -0.5)
    k = k * lax.rsqrt(_split_mm(k * k, bd) + EPS)

    sm = small_ref[...]
    gp = gpar_ref[...]
    g = -jnp.exp(gp[0:1, :]) * _softplus(sm + gp[1:2, :])
    beta = _sigmoid(sm)
    gcum = _cumsum_rows(g, DN_CHUNK)
    gcum_t = gcum.T
    egcum = jnp.exp(gcum)

    r_i = lax.broadcasted_iota(jnp.int32, (DN_CHUNK, DN_CHUNK), 0)
    c_i = lax.broadcasted_iota(jnp.int32, (DN_CHUNK, DN_CHUNK), 1)
    causal = r_i >= c_i
    strict = r_i > c_i

    for c in range(tc // DN_CHUNK):
        r0 = c * DN_CHUNK
        r1 = r0 + DN_CHUNK
        for h in range(DN_HEADS):
            c0 = h * HEAD_DIM
            c1 = c0 + HEAD_DIM
            qh = q[r0:r1, c0:c1]
            kh = k[r0:r1, c0:c1]
            vh = v[r0:r1, c0:c1]
            gc = gcum[r0:r1, LANE_A + h:LANE_A + h + 1]
            gr = gcum_t[LANE_A + h:LANE_A + h + 1, r0:r1]
            bc = beta[r0:r1, LANE_B + h:LANE_B + h + 1]
            egc = egcum[r0:r1, LANE_A + h:LANE_A + h + 1]
            glast = gcum[r1 - 1:r1, LANE_A + h:LANE_A + h + 1]

            decay = jnp.where(causal, jnp.exp(jnp.where(causal, gc - gr, 0.0)), 0.0)
            kb = kh * bc
            prod = _mm_nt(jnp.concatenate([qh, kb], axis=0), kh)
            qk = prod[0:DN_CHUNK] * decay
            a_mat = jnp.where(strict, prod[DN_CHUNK:2 * DN_CHUNK] * decay, 0.0)
            t_inv = _unit_lower_inverse(a_mat, r_i, c_i)
            uw = _mm(t_inv, jnp.concatenate([vh * bc, kb * egc], axis=1))
            u = uw[:, 0:HEAD_DIM]
            wk = uw[:, HEAD_DIM:2 * HEAD_DIM]

            s_old = s_ref[h]
            ws = _mm(jnp.concatenate([wk, qh * egc], axis=0), s_old)
            v_new = u - ws[0:DN_CHUNK]
            o = ws[DN_CHUNK:2 * DN_CHUNK] + _mm(qk, v_new)
            k_dec = kh * jnp.exp(glast - gc)
            s_ref[h] = s_old * jnp.exp(glast) + _mm_tn(k_dec, v_new)
            oscr_ref[r0:r1, c0:c1] = o

    o = oscr_ref[...]
    ms = _split_mm(o * o, bd) * (1.0 / HEAD_DIM)
    o = o * lax.rsqrt(ms + EPS) * onorm_ref[...]
    zz = z_ref[...].astype(F32)
    o_ref[...] = (o * (zz * _sigmoid(zz))).astype(BF16)


def _deltanet(dn_qkv, dn_z, small, conv_w, gpar, onorm, bd, batch, seq, tc):
    n = batch * seq
    nt = seq // tc
    row = lambda w: pl.BlockSpec((tc, w), lambda b, i: (b * nt + i, 0))
    full = lambda a: pl.BlockSpec(a.shape, lambda b, i: (0,) * a.ndim)
    return pl.pallas_call(
        functools.partial(_dn_kernel, tc=tc),
        grid=(batch, nt),
        in_specs=[row(QKV_W), row(DN_WIDTH), row(LANES), full(conv_w), full(gpar), full(onorm), full(bd)],
        out_specs=row(DN_WIDTH),
        out_shape=jax.ShapeDtypeStruct((n, DN_WIDTH), BF16),
        scratch_shapes=[
            pltpu.VMEM((tc + 8, QKV_W), F32),
            pltpu.VMEM((DN_HEADS, HEAD_DIM, HEAD_DIM), F32),
            pltpu.VMEM((tc, DN_WIDTH), F32),
        ],
        compiler_params=pltpu.CompilerParams(dimension_semantics=("arbitrary", "arbitrary"),
                                             vmem_limit_bytes=VMEM_LIMIT),
        name="deltanet",
    )(dn_qkv, dn_z, small, conv_w, gpar, onorm, bd)


def _fxprep_kernel(qkv_ref, small_ref, fbias_ref, qg_ref, kg_ref, bd_ref, qa_ref, ka_ref, carry_ref, *, tc):
    i = pl.program_id(1)

    @pl.when(i == 0)
    def _():
        carry_ref[...] = jnp.zeros_like(carry_ref)

    bd = bd_ref[...]
    q = qkv_ref[:, 0:FX_WIDTH].astype(F32)
    k = qkv_ref[:, FX_WIDTH:2 * FX_WIDTH].astype(F32)
    q = q * lax.rsqrt(_split_mm(q * q, bd) * (1.0 / HEAD_DIM) + EPS) * qg_ref[...]
    k = k * lax.rsqrt(_split_mm(k * k, bd) * (1.0 / HEAD_DIM) + EPS) * kg_ref[...]

    logf = -_softplus(-(small_ref[...] + fbias_ref[...]))
    c = _cumsum_rows(logf, tc) + carry_ref[...]
    carry_ref[...] = c[tc - 1:tc, :]

    c_hi = c.astype(BF16).astype(F32)
    r1 = c - c_hi
    c_mid = r1.astype(BF16).astype(F32)
    c_lo = (r1 - c_mid).astype(BF16).astype(F32)

    li = lax.broadcasted_iota(jnp.int32, (tc, HEAD_DIM), 1)
    for h in range(FX_HEADS):
        lane = LANE_F + h
        shape = (tc, HEAD_DIM)
        hi = jnp.broadcast_to(c_hi[:, lane:lane + 1], shape)
        mid = jnp.broadcast_to(c_mid[:, lane:lane + 1], shape)
        lo = jnp.broadcast_to(c_lo[:, lane:lane + 1], shape)
        q_ext = jnp.where(li == 0, hi, jnp.where(li == 1, mid, jnp.where(li == 2, lo,
                          jnp.where(li < 6, 1.0, 0.0))))
        k_ext = jnp.where(li < 3, 1.0, jnp.where(li == 3, -hi, jnp.where(li == 4, -mid,
                          jnp.where(li == 5, -lo, 0.0))))
        c0 = h * HEAD_DIM
        qa_ref[:, h * LANES:(h + 1) * LANES] = jnp.concatenate(
            [q[:, c0:c0 + HEAD_DIM], q_ext], axis=1).astype(BF16)
        ka_ref[:, h * LANES:(h + 1) * LANES] = jnp.concatenate(
            [k[:, c0:c0 + HEAD_DIM], k_ext], axis=1).astype(BF16)


def _fxprep(fx_qkv, small, fbias, qg, kg, bd, batch, seq, tc):
    n = batch * seq
    nt = seq // tc
    row = lambda w: pl.BlockSpec((tc, w), lambda b, i: (b * nt + i, 0))
    full = lambda a: pl.BlockSpec(a.shape, lambda b, i: (0,) * a.ndim)
    out = jax.ShapeDtypeStruct((n, FX_HEADS * LANES), BF16)
    return pl.pallas_call(
        functools.partial(_fxprep_kernel, tc=tc),
        grid=(batch, nt),
        in_specs=[row(QKV_W), row(LANES), full(fbias), full(qg), full(kg), full(bd)],
        out_specs=(row(FX_HEADS * LANES), row(FX_HEADS * LANES)),
        out_shape=(out, out),
        scratch_shapes=[pltpu.VMEM((1, LANES), F32)],
        compiler_params=pltpu.CompilerParams(dimension_semantics=("arbitrary", "arbitrary"),
                                             vmem_limit_bytes=VMEM_LIMIT),
        name="fxprep",
    )(fx_qkv, small, fbias, qg, kg, bd)


def _fxattn_kernel(qi_ref, kj_ref, qa_ref, ka_ref, v_ref, o_ref, m_ref, l_ref, acc_ref, *, tq):
    p = pl.program_id(2)
    i = qi_ref[p]
    j = kj_ref[p]

    @pl.when(j == 0)
    def _():
        m_ref[...] = jnp.full_like(m_ref, NEG_BIG)
        l_ref[...] = jnp.zeros_like(l_ref)
        acc_ref[...] = jnp.zeros_like(acc_ref)

    def step(masked):
        for hh in range(2):
            qa = qa_ref[:, hh * LANES:(hh + 1) * LANES]
            ka = ka_ref[:, hh * LANES:(hh + 1) * LANES]
            s = lax.dot_general(ka, qa, (((1,), (1,)), ((), ())), preferred_element_type=F32)
            if masked:
                row = lax.broadcasted_iota(jnp.int32, s.shape, 0)
                col = lax.broadcasted_iota(jnp.int32, s.shape, 1)
                s = jnp.where(row <= col, s, NEG_BIG)
            m_prev = m_ref[hh]
            m_new = jnp.maximum(m_prev, jnp.max(s, axis=0, keepdims=True))
            alpha = jnp.exp(m_prev - m_new)
            pexp = jnp.exp(s - m_new)
            l_ref[hh] = alpha * l_ref[hh] + jnp.sum(pexp, axis=0, keepdims=True)
            vh = v_ref[:, hh * HEAD_DIM:(hh + 1) * HEAD_DIM]
            pv = lax.dot_general(vh, pexp.astype(BF16), (((0,), (0,)), ((), ())),
                                 preferred_element_type=F32)
            acc_ref[hh] = alpha * acc_ref[hh] + pv
            m_ref[hh] = m_new

    @pl.when(j < i)
    def _():
        step(False)

    @pl.when(j == i)
    def _():
        step(True)
        outs = [(acc_ref[hh] / l_ref[hh]).T for hh in range(2)]
        o_ref[...] = jnp.concatenate(outs, axis=1).astype(BF16)


def _fxattn(qa, ka, fx_qkv, batch, seq, tq):
    n = batch * seq
    nq = seq // tq
    pairs = [(i, j) for i in range(nq) for j in range(i + 1)]
    qi = jnp.asarray(np.array([p[0] for p in pairs], np.int32))
    kj = jnp.asarray(np.array([p[1] for p in pairs], np.int32))
    v_col0 = (2 * FX_WIDTH) // LANES
    grid_spec = pltpu.PrefetchScalarGridSpec(
        num_scalar_prefetch=2,
        grid=(batch, FX_HEADS // 2, len(pairs)),
        in_specs=[
            pl.BlockSpec((tq, 2 * LANES), lambda b, hp, p, qi, kj: (b * nq + qi[p], hp)),
            pl.BlockSpec((tq, 2 * LANES), lambda b, hp, p, qi, kj: (b * nq + kj[p], hp)),
            pl.BlockSpec((tq, LANES), lambda b, hp, p, qi, kj: (b * nq + kj[p], v_col0 + hp)),
        ],
        out_specs=pl.BlockSpec((tq, LANES), lambda b, hp, p, qi, kj: (b * nq + qi[p], hp)),
        scratch_shapes=[
            pltpu.VMEM((2, 1, tq), F32),
            pltpu.VMEM((2, 1, tq), F32),
            pltpu.VMEM((2, HEAD_DIM, tq), F32),
        ],
    )
    return pl.pallas_call(
        functools.partial(_fxattn_kernel, tq=tq),
        grid_spec=grid_spec,
        out_shape=jax.ShapeDtypeStruct((n, FX_WIDTH), BF16),
        compiler_params=pltpu.CompilerParams(
            dimension_semantics=("arbitrary", "arbitrary", "arbitrary"),
            vmem_limit_bytes=VMEM_LIMIT),
        name="fxattn",
    )(qi, kj, qa, ka, fx_qkv)


def _outproj_kernel(x_ref, odn_ref, ofx_ref, px_ref, wout_ref, pbd_ref, pscale_ref, xo_ref, ext_ref, *, tc):
    i = pl.program_id(1)
    wmax = POOL_WINDOWS[-1]

    @pl.when(i == 0)
    def _():
        ext_ref[0:wmax, :] = jnp.zeros((wmax, POOL_WIDTH), F32)

    xp = px_ref[...].astype(F32)
    ext_ref[wmax:wmax + tc, :] = xp
    acc = xp
    sums = {}
    for j in range(1, wmax):
        acc = acc + ext_ref[pl.ds(wmax - j, tc), :]
        if j + 1 in POOL_WINDOWS:
            sums[j + 1] = acc
    ext_ref[0:wmax, :] = xp[tc - wmax:tc, :]

    grp = lax.broadcasted_iota(jnp.int32, (tc, POOL_WIDTH), 1) // POOL_GROUP_DIM
    t1 = lax.broadcasted_iota(jnp.int32, (tc, POOL_WIDTH), 0) + (i * tc + 1)
    wsum = sums[POOL_WINDOWS[-1]]
    wlen = jnp.full((tc, POOL_WIDTH), POOL_WINDOWS[-1], jnp.int32)
    for gi in range(POOL_GROUPS - 2, -1, -1):
        wsum = jnp.where(grp == gi, sums[POOL_WINDOWS[gi]], wsum)
        wlen = jnp.where(grp == gi, POOL_WINDOWS[gi], wlen)
    count = jnp.minimum(t1, wlen).astype(F32)
    y = wsum / count - xp
    pooled = _mm(y, pbd_ref[...]) * pscale_ref[...]

    out = x_ref[...]
    out = out + jnp.dot(odn_ref[...], wout_ref[0:DN_WIDTH, :], preferred_element_type=F32)
    out = out + jnp.dot(ofx_ref[...], wout_ref[DN_WIDTH:DN_WIDTH + FX_WIDTH, :], preferred_element_type=F32)
    out = out + jnp.dot(pooled.astype(BF16), wout_ref[DN_WIDTH + FX_WIDTH:D_MODEL, :],
                        preferred_element_type=F32)
    xo_ref[...] = out


def _outproj(xf, o_dn, o_fx, pool_x, w_out, pool_bd, pool_scale, batch, seq, tc):
    n = batch * seq
    nt = seq // tc
    row = lambda w: pl.BlockSpec((tc, w), lambda b, i: (b * nt + i, 0))
    full = lambda a: pl.BlockSpec(a.shape, lambda b, i: (0,) * a.ndim)
    return pl.pallas_call(
        functools.partial(_outproj_kernel, tc=tc),
        grid=(batch, nt),
        in_specs=[row(D_MODEL), row(DN_WIDTH), row(FX_WIDTH), row(POOL_WIDTH),
                  full(w_out), full(pool_bd), full(pool_scale)],
        out_specs=row(D_MODEL),
        out_shape=jax.ShapeDtypeStruct((n, D_MODEL), F32),
        scratch_shapes=[pltpu.VMEM((tc + POOL_WINDOWS[-1], POOL_WIDTH), F32)],
        compiler_params=pltpu.CompilerParams(dimension_semantics=("arbitrary", "arbitrary"),
                                             vmem_limit_bytes=VMEM_LIMIT),
        name="outproj",
    )(xf, o_dn, o_fx, pool_x, w_out, pool_bd, pool_scale)


def _ffn_kernel(x_ref, g_ref, wg_ref, wu_ref, wd_ref, o_ref, h_ref, acc_ref):
    f = pl.program_id(1)

    @pl.when(f == 0)
    def _():
        x = x_ref[...]
        ms = jnp.mean(x * x, axis=-1, keepdims=True)
        h_ref[...] = (x * lax.rsqrt(ms + EPS) * g_ref[...]).astype(BF16)
        acc_ref[...] = x

    h = h_ref[...]
    a = jnp.dot(h, wg_ref[...], preferred_element_type=F32)
    u = jnp.dot(h, wu_ref[...], preferred_element_type=F32)
    t = (a * _sigmoid(a) * u).astype(BF16)
    acc_ref[...] += jnp.dot(t, wd_ref[...], preferred_element_type=F32)

    @pl.when(f == pl.num_programs(1) - 1)
    def _():
        o_ref[...] = acc_ref[...]


def _ffn(xf, g, w_gate, w_up, w_down, tm, tf):
    n = xf.shape[0]
    nf = D_FF // tf
    return pl.pallas_call(
        _ffn_kernel,
        grid=(n // tm, nf),
        in_specs=[
            pl.BlockSpec((tm, D_MODEL), lambda i, f: (i, 0)),
            pl.BlockSpec((1, D_MODEL), lambda i, f: (0, 0)),
            pl.BlockSpec((D_MODEL, tf), lambda i, f: (0, f)),
            pl.BlockSpec((D_MODEL, tf), lambda i, f: (0, f)),
            pl.BlockSpec((tf, D_MODEL), lambda i, f: (f, 0)),
        ],
        out_specs=pl.BlockSpec((tm, D_MODEL), lambda i, f: (i, 0)),
        out_shape=jax.ShapeDtypeStruct((n, D_MODEL), F32),
        scratch_shapes=[pltpu.VMEM((tm, D_MODEL), BF16), pltpu.VMEM((tm, D_MODEL), F32)],
        compiler_params=pltpu.CompilerParams(dimension_semantics=("arbitrary", "arbitrary"),
                                             vmem_limit_bytes=VMEM_LIMIT),
        name="ffn",
    )(xf, g, w_gate, w_up, w_down)


def _moe_kernel(x_ref, g_ref, rhi_ref, rlo_ref, wg_ref, wu_ref, wd_ref, o_ref, h_ref, gate_ref, acc_ref):
    e = pl.program_id(1)
    tm = x_ref.shape[0]
    lane = lax.broadcasted_iota(jnp.int32, (tm, LANES), 1)

    @pl.when(e == 0)
    def _():
        x = x_ref[...]
        ms = jnp.mean(x * x, axis=-1, keepdims=True)
        hf = x * lax.rsqrt(ms + EPS) * g_ref[...]
        h_hi = hf.astype(BF16)
        h_lo = (hf - h_hi.astype(F32)).astype(BF16)
        h_ref[...] = h_hi
        acc_ref[...] = x
        logits = (jnp.dot(h_hi, rhi_ref[...], preferred_element_type=F32)
                  + jnp.dot(h_lo, rhi_ref[...], preferred_element_type=F32)
                  + jnp.dot(h_hi, rlo_ref[...], preferred_element_type=F32))
        logits = jnp.where(lane < N_EXPERTS, logits, NEG_BIG)
        ex = jnp.exp(logits - jnp.max(logits, axis=-1, keepdims=True))
        probs = ex / jnp.sum(ex, axis=-1, keepdims=True)
        p1 = jnp.max(probs, axis=-1, keepdims=True)
        i1 = jnp.min(jnp.where(probs == p1, lane, LANES), axis=-1, keepdims=True)
        rest = jnp.where(lane == i1, -1.0, probs)
        p2 = jnp.max(rest, axis=-1, keepdims=True)
        i2 = jnp.min(jnp.where(rest == p2, lane, LANES), axis=-1, keepdims=True)
        denom = p1 + p2
        gate_ref[...] = (jnp.where(lane == i1, p1 / denom, 0.0)
                         + jnp.where(lane == i2, p2 / denom, 0.0))

    h = h_ref[...]
    a = jnp.dot(h, wg_ref[...], preferred_element_type=F32)
    u = jnp.dot(h, wu_ref[...], preferred_element_type=F32)
    t = (a * _sigmoid(a) * u).astype(BF16)
    y = jnp.dot(t, wd_ref[...], preferred_element_type=F32)
    gate_e = jnp.sum(jnp.where(lane == e, gate_ref[...], 0.0), axis=-1, keepdims=True)
    acc_ref[...] += gate_e * y

    @pl.when(e == pl.num_programs(1) - 1)
    def _():
        o_ref[...] = acc_ref[...]


def _moe(xf, g, r_hi, r_lo, w_gate, w_up, w_down, tm):
    n = xf.shape[0]
    return pl.pallas_call(
        _moe_kernel,
        grid=(n // tm, N_EXPERTS),
        in_specs=[
            pl.BlockSpec((tm, D_MODEL), lambda i, e: (i, 0)),
            pl.BlockSpec((1, D_MODEL), lambda i, e: (0, 0)),
            pl.BlockSpec((D_MODEL, LANES), lambda i, e: (0, 0)),
            pl.BlockSpec((D_MODEL, LANES), lambda i, e: (0, 0)),
            pl.BlockSpec((None, D_MODEL, D_FF_EXPERT), lambda i, e: (e, 0, 0)),
            pl.BlockSpec((None, D_MODEL, D_FF_EXPERT), lambda i, e: (e, 0, 0)),
            pl.BlockSpec((None, D_FF_EXPERT, D_MODEL), lambda i, e: (e, 0, 0)),
        ],
        out_specs=pl.BlockSpec((tm, D_MODEL), lambda i, e: (i, 0)),
        out_shape=jax.ShapeDtypeStruct((n, D_MODEL), F32),
        scratch_shapes=[pltpu.VMEM((tm, D_MODEL), BF16), pltpu.VMEM((tm, LANES), F32),
                        pltpu.VMEM((tm, D_MODEL), F32)],
        compiler_params=pltpu.CompilerParams(dimension_semantics=("arbitrary", "arbitrary"),
                                             vmem_limit_bytes=VMEM_LIMIT),
        name="moe",
    )(xf, g, r_hi, r_lo, w_gate, w_up, w_down)


def _pad_lanes(v, offset):
    return jnp.zeros((1, LANES), F32).at[0, offset:offset + v.shape[0]].set(v.astype(F32))


def _block_diag_ones(width, group):
    idx = np.arange(width) // group
    return jnp.asarray((idx[:, None] == idx[None, :]).astype(np.float32), BF16)


def _tile(n, pref):
    return pref if n % pref == 0 else n


def _mixer(xf, batch, seq, norm1, w_in, dn_conv, dn_a_log, dn_dt_bias, dn_onorm, fx_qnorm, fx_knorm,
           fx_f_bias, pool_w, pool_scale, w_out):
    n = batch * seq
    o = 0
    sizes = (DN_WIDTH, DN_WIDTH, DN_WIDTH, DN_WIDTH, DN_HEADS, DN_HEADS,
             FX_WIDTH, FX_WIDTH, FX_WIDTH, FX_HEADS, POOL_WIDTH)
    cols = []
    for s in sizes:
        cols.append((o, o + s))
        o += s
    sl = lambda idx: w_in[:, cols[idx][0]:cols[idx][1]]
    w_big = jnp.concatenate([sl(0), sl(1), sl(2), sl(3), sl(6), sl(7), sl(8), sl(10)], axis=1).astype(BF16)
    w_small = jnp.concatenate(
        [sl(4), sl(5), sl(9), jnp.zeros((D_MODEL, LANES - 3 * DN_HEADS), F32)], axis=1).astype(BF16)

    dn_qkv, dn_z, fx_qkv, pool_x, small = _inproj(
        xf, norm1.reshape(1, D_MODEL), w_big, w_small, _tile(n, 512))

    bd = _block_diag_ones(DN_WIDTH, HEAD_DIM)
    gpar = jnp.concatenate([_pad_lanes(dn_a_log, LANE_A), _pad_lanes(dn_dt_bias, LANE_A)], axis=0)
    o_dn = _deltanet(dn_qkv, dn_z, small, dn_conv, gpar, jnp.tile(dn_onorm, DN_HEADS).reshape(1, DN_WIDTH),
                     bd, batch, seq, _tile(seq, 256))

    qg = (jnp.tile(fx_qnorm, FX_HEADS) * (HEAD_DIM ** -0.5)).reshape(1, FX_WIDTH)
    kg = jnp.tile(fx_knorm, FX_HEADS).reshape(1, FX_WIDTH)
    qa, ka = _fxprep(fx_qkv, small, _pad_lanes(fx_f_bias, LANE_F), qg, kg, bd, batch, seq, _tile(seq, 512))
    o_fx = _fxattn(qa, ka, fx_qkv, batch, seq, _tile(seq, 512))

    pool_bd = jax.scipy.linalg.block_diag(*[pool_w[gi] for gi in range(POOL_GROUPS)]).astype(BF16)
    return _outproj(xf, o_dn, o_fx, pool_x, w_out.astype(BF16), pool_bd,
                    pool_scale.reshape(1, POOL_WIDTH), batch, seq, _tile(seq, 512))


def kernel(x, norm1, w_in, dn_conv, dn_a_log, dn_dt_bias, dn_onorm, fx_qnorm, fx_knorm, fx_f_bias, pool_w,
           pool_scale, w_out, norm2, ffn_gate, ffn_up, ffn_down, router, moe_gate, moe_up, moe_down):
    batch, seq, _ = x.shape
    n = batch * seq
    xf = x.reshape(n, D_MODEL)
    depth = norm1.shape[0]
    for layer in range(depth):
        xf = _mixer(xf, batch, seq, norm1[layer], w_in[layer], dn_conv[layer], dn_a_log[layer],
                    dn_dt_bias[layer], dn_onorm[layer], fx_qnorm[layer], fx_knorm[layer],
                    fx_f_bias[layer], pool_w[layer], pool_scale[layer], w_out[layer])
        g2 = norm2[layer].reshape(1, D_MODEL)
        j = layer // 2
        if layer % 2 == 0:
            xf = _ffn(xf, g2, ffn_gate[j].astype(BF16), ffn_up[j].astype(BF16), ffn_down[j].astype(BF16),
                      _tile(n, 512), D_FF // 2)
        else:
            r = jnp.concatenate([router[j], jnp.zeros((D_MODEL, LANES - N_EXPERTS), F32)], axis=1)
            r_hi = r.astype(BF16)
            r_lo = (r - r_hi.astype(F32)).astype(BF16)
            xf = _moe(xf, g2, r_hi, r_lo, moe_gate[j].astype(BF16), moe_up[j].astype(BF16),
                      moe_down[j].astype(BF16), _tile(n, 1024))
    return xf.reshape(batch, seq, D_MODEL)
```

```python
import functools

import jax
import jax.numpy as jnp
import numpy as np
from jax import lax
from jax.experimental import pallas as pl
from jax.experimental.pallas import tpu as pltpu

F32 = jnp.float32
BF16 = jnp.bfloat16

D_MODEL = 1024
HEAD_DIM = 64
DN_HEADS = 6
DN_WIDTH = DN_HEADS * HEAD_DIM
FX_HEADS = 6
FX_WIDTH = FX_HEADS * HEAD_DIM
POOL_GROUPS = 4
POOL_GROUP_DIM = 64
POOL_WIDTH = POOL_GROUPS * POOL_GROUP_DIM
POOL_WINDOWS = (2, 4, 8, 16)
CONV_WIDTH = 4
DN_CHUNK = 64
D_FF = 2816
N_EXPERTS = 8
D_FF_EXPERT = 1536
EPS = 1e-6
LANES = 128
NEG_BIG = -1e30

QKV_W = 3 * DN_WIDTH
COL_DN_QKV = 0
COL_DN_Z = COL_DN_QKV + QKV_W
COL_FX_QKV = COL_DN_Z + DN_WIDTH
COL_POOL = COL_FX_QKV + QKV_W
N_BIG = COL_POOL + POOL_WIDTH
LANE_A = 0
LANE_B = DN_HEADS
LANE_F = 2 * DN_HEADS

VMEM_LIMIT = 56 * 1024 * 1024


def _mm(a, b):
    return jnp.dot(a.astype(BF16), b.astype(BF16), preferred_element_type=F32)


def _mm_nt(a, b):
    return lax.dot_general(a.astype(BF16), b.astype(BF16), (((1,), (1,)), ((), ())),
                           preferred_element_type=F32)


def _mm_tn(a, b):
    return lax.dot_general(a.astype(BF16), b.astype(BF16), (((0,), (0,)), ((), ())),
                           preferred_element_type=F32)


def _split_mm(a, b_bf16):
    hi = a.astype(BF16)
    lo = (a - hi.astype(F32)).astype(BF16)
    return (jnp.dot(hi, b_bf16, preferred_element_type=F32)
            + jnp.dot(lo, b_bf16, preferred_element_type=F32))


def _sigmoid(x):
    return 1.0 / (1.0 + jnp.exp(-x))


def _softplus(x):
    return jnp.maximum(x, 0.0) + jnp.log(1.0 + jnp.exp(-jnp.abs(x)))


def _cumsum_rows(x, period):
    ridx = lax.broadcasted_iota(jnp.int32, x.shape, 0) & (period - 1)
    s = 1
    while s < period:
        x = x + jnp.where(ridx >= s, pltpu.roll(x, s, axis=0), 0.0)
        s *= 2
    return x


def _inproj_kernel(x_ref, g_ref, w_ref, ws_ref, dnqkv_ref, dnz_ref, fxqkv_ref, pool_ref, small_ref):
    x = x_ref[...]
    ms = jnp.mean(x * x, axis=-1, keepdims=True)
    h = (x * lax.rsqrt(ms + EPS) * g_ref[...]).astype(BF16)

    def proj(lo, hi):
        return jnp.dot(h, w_ref[:, lo:hi], preferred_element_type=F32)

    dnqkv_ref[...] = proj(COL_DN_QKV, COL_DN_Z).astype(BF16)
    dnz_ref[...] = proj(COL_DN_Z, COL_FX_QKV).astype(BF16)
    fxqkv_ref[...] = proj(COL_FX_QKV, COL_POOL).astype(BF16)
    pool_ref[...] = proj(COL_POOL, N_BIG).astype(BF16)
    small_ref[...] = jnp.dot(h, ws_ref[...], preferred_element_type=F32)


def _inproj(xf, g, w_big, w_small, tm):
    n = xf.shape[0]
    out_shape = (
        jax.ShapeDtypeStruct((n, QKV_W), BF16),
        jax.ShapeDtypeStruct((n, DN_WIDTH), BF16),
        jax.ShapeDtypeStruct((n, QKV_W), BF16),
        jax.ShapeDtypeStruct((n, POOL_WIDTH), BF16),
        jax.ShapeDtypeStruct((n, LANES), F32),
    )
    row = lambda w: pl.BlockSpec((tm, w), lambda i: (i, 0))
    full = lambda a: pl.BlockSpec(a.shape, lambda i: (0,) * a.ndim)
    return pl.pallas_call(
        _inproj_kernel,
        grid=(n // tm,),
        in_specs=[row(D_MODEL), full(g), full(w_big), full(w_small)],
        out_specs=(row(QKV_W), row(DN_WIDTH), row(QKV_W), row(POOL_WIDTH), row(LANES)),
        out_shape=out_shape,
        compiler_params=pltpu.CompilerParams(dimension_semantics=("arbitrary",),
                                             vmem_limit_bytes=VMEM_LIMIT),
        name="inproj",
    )(xf, g, w_big, w_small)


def _dotf(a, b):
    return jnp.dot(a, b, preferred_element_type=F32)


def _unit_lower_inverses(a_list, masks_ref, eye):
    m8 = masks_ref[0]
    d = [a * m8 for a in a_list]
    db = [x.astype(BF16) for x in d]
    d2 = [_dotf(x, x).astype(BF16) for x in db]
    d4 = [_dotf(x, x).astype(BF16) for x in d2]
    x = [eye - dd for dd in d]
    x = [xx + _dotf(xx.astype(BF16), y) for xx, y in zip(x, d2)]
    x = [xx + _dotf(xx.astype(BF16), y) for xx, y in zip(x, d4)]
    for level in (1, 2, 3):
        me = masks_ref[level]
        e = [(a * me).astype(BF16) for a in a_list]
        xb = [xx.astype(BF16) for xx in x]
        ex = [_dotf(ee, xx).astype(BF16) for ee, xx in zip(e, xb)]
        x = [xx - _dotf(xxb, eex) for xx, xxb, eex in zip(x, xb, ex)]
    return x


def _dn_kernel(qkv_ref, z_ref, small_ref, convw_ref, gpar_ref, onorm_ref, bd_ref, ea_ref, eb_ref, masks_ref,
               o_ref, ext_ref, s_ref, oscr_ref, *, tc):
    i = pl.program_id(1)
    n_chunks = tc // DN_CHUNK
    heads = range(DN_HEADS)

    @pl.when(i == 0)
    def _():
        ext_ref[0:8, :] = jnp.zeros((8, QKV_W), F32)
        s_ref[...] = jnp.zeros_like(s_ref)

    x = qkv_ref[...].astype(F32)
    ext_ref[8:8 + tc, :] = x
    w = convw_ref[...]
    y = x * w[CONV_WIDTH - 1:CONV_WIDTH, :]
    for j in range(CONV_WIDTH - 1):
        y = y + ext_ref[pl.ds(8 - (CONV_WIDTH - 1) + j, tc), :] * w[j:j + 1, :]
    ext_ref[0:8, :] = x[tc - 8:tc, :]
    y = y * _sigmoid(y)

    bd = bd_ref[...]
    q = y[:, 0:DN_WIDTH]
    k = y[:, DN_WIDTH:2 * DN_WIDTH]
    v = y[:, 2 * DN_WIDTH:3 * DN_WIDTH]
    q = q * lax.rsqrt(_split_mm(q * q, bd) + EPS) * (HEAD_DIM ** -0.5)
    k = k * lax.rsqrt(_split_mm(k * k, bd) + EPS)

    sm = small_ref[...]
    gp = gpar_ref[...]
    g = -jnp.exp(gp[0:1, :]) * _softplus(sm + gp[1:2, :])
    beta = _sigmoid(sm)
    gcum = _cumsum_rows(g, DN_CHUNK)
    chunk_of_row = lax.broadcasted_iota(jnp.int32, (tc, LANES), 0) // DN_CHUNK
    glast = jnp.broadcast_to(gcum[tc - 1:tc, :], (tc, LANES))
    for c in range(n_chunks - 2, -1, -1):
        glast = jnp.where(chunk_of_row == c, gcum[(c + 1) * DN_CHUNK - 1:(c + 1) * DN_CHUNK, :], glast)

    ea = ea_ref[...]
    eb = eb_ref[...]
    beta_w = _split_mm(beta, eb)
    eg_w = _split_mm(jnp.exp(gcum), ea)
    ekd_w = _split_mm(jnp.exp(glast - gcum), ea)
    egl_w = _split_mm(jnp.exp(glast), ea)
    kb = k * beta_w
    vb = v * beta_w
    kbg = kb * eg_w
    q_dec = q * eg_w
    k_dec = k * ekd_w
    gcum_t = gcum.T

    def hs(a, h):
        return a[:, h * HEAD_DIM:(h + 1) * HEAD_DIM]

    r_i = lax.broadcasted_iota(jnp.int32, (tc, tc), 0)
    c_i = lax.broadcasted_iota(jnp.int32, (tc, tc), 1)
    same_chunk = (r_i // DN_CHUNK) == (c_i // DN_CHUNK)
    causal = jnp.logical_and(same_chunk, r_i >= c_i)
    eye = jnp.where(r_i == c_i, 1.0, 0.0)

    decay = []
    for h in heads:
        diff = gcum[:, LANE_A + h:LANE_A + h + 1] - gcum_t[LANE_A + h:LANE_A + h + 1, :]
        decay.append(jnp.where(causal, jnp.exp(jnp.where(causal, diff, 0.0)), 0.0))
    prod = [_mm_nt(jnp.concatenate([hs(q, h), hs(kb, h)], axis=0), hs(k, h)) for h in heads]
    qk = [(prod[h][0:tc] * decay[h]).astype(BF16) for h in heads]
    a_mat = [prod[h][tc:2 * tc] * decay[h] * (1.0 - eye) for h in heads]
    t_inv = _unit_lower_inverses(a_mat, masks_ref, eye)
    uw = [_dotf(t_inv[h].astype(BF16),
                jnp.concatenate([hs(vb, h), hs(kbg, h)], axis=1).astype(BF16)).astype(BF16) for h in heads]
    k_bd = [jnp.where(same_chunk, jnp.concatenate([hs(k_dec, h)] * n_chunks, axis=1), 0.0).astype(BF16)
            for h in heads]
    ktuw = [lax.dot_general(k_bd[h], uw[h], (((0,), (0,)), ((), ())), preferred_element_type=F32)
            for h in heads]
    qkuw = [_dotf(qk[h], uw[h]) for h in heads]

    for c in range(n_chunks):
        r0 = c * DN_CHUNK
        r1 = r0 + DN_CHUNK
        s_all = s_ref[...]
        lhs = [jnp.concatenate([hs(q_dec, h)[r0:r1] - qkuw[h][r0:r1, HEAD_DIM:2 * HEAD_DIM],
                                ktuw[h][r0:r1, HEAD_DIM:2 * HEAD_DIM]], axis=0) for h in heads]
        res = [_mm(lhs[h], hs(s_all, h)) for h in heads]
        oscr_ref[r0:r1, :] = jnp.concatenate(
            [res[h][0:DN_CHUNK] + qkuw[h][r0:r1, 0:HEAD_DIM] for h in heads], axis=1)
        s_ref[...] = s_all * egl_w[r0:r0 + 1, :] + jnp.concatenate(
            [ktuw[h][r0:r1, 0:HEAD_DIM] - res[h][DN_CHUNK:2 * DN_CHUNK] for h in heads], axis=1)

    o = oscr_ref[...]
    ms = _split_mm(o * o, bd) * (1.0 / HEAD_DIM)
    o = o * lax.rsqrt(ms + EPS) * onorm_ref[...]
    zz = z_ref[...].astype(F32)
    o_ref[...] = (o * (zz * _sigmoid(zz))).astype(BF16)


def _dn_constants(tc):
    ea = np.zeros((LANES, DN_WIDTH), np.float32)
    eb = np.zeros((LANES, DN_WIDTH), np.float32)
    for h in range(DN_HEADS):
        ea[LANE_A + h, h * HEAD_DIM:(h + 1) * HEAD_DIM] = 1.0
        eb[LANE_B + h, h * HEAD_DIM:(h + 1) * HEAD_DIM] = 1.0
    r = np.arange(tc)[:, None]
    c = np.arange(tc)[None, :]
    masks = [(r >> 3) == (c >> 3)]
    for s in (3, 4, 5):
        masks.append(((r >> s) ^ (c >> s)) == 1)
    masks = np.stack(masks).astype(np.float32)
    return jnp.asarray(ea, BF16), jnp.asarray(eb, BF16), jnp.asarray(masks, F32)


def _deltanet(dn_qkv, dn_z, small, conv_w, gpar, onorm, bd, batch, seq, tc):
    n = batch * seq
    nt = seq // tc
    ea, eb, masks = _dn_constants(tc)
    row = lambda w: pl.BlockSpec((tc, w), lambda b, i: (b * nt + i, 0))
    full = lambda a: pl.BlockSpec(a.shape, lambda b, i: (0,) * a.ndim)
    return pl.pallas_call(
        functools.partial(_dn_kernel, tc=tc),
        grid=(batch, nt),
        in_specs=[row(QKV_W), row(DN_WIDTH), row(LANES), full(conv_w), full(gpar), full(onorm), full(bd),
                  full(ea), full(eb), full(masks)],
        out_specs=row(DN_WIDTH),
        out_shape=jax.ShapeDtypeStruct((n, DN_WIDTH), BF16),
        scratch_shapes=[
            pltpu.VMEM((tc + 8, QKV_W), F32),
            pltpu.VMEM((HEAD_DIM, DN_WIDTH), F32),
            pltpu.VMEM((tc, DN_WIDTH), F32),
        ],
        compiler_params=pltpu.CompilerParams(dimension_semantics=("arbitrary", "arbitrary"),
                                             vmem_limit_bytes=VMEM_LIMIT),
        name="deltanet",
    )(dn_qkv, dn_z, small, conv_w, gpar, onorm, bd, ea, eb, masks)


def _fxprep_kernel(qkv_ref, small_ref, fbias_ref, qg_ref, kg_ref, bd_ref, qa_ref, ka_ref, carry_ref, *, tc):
    i = pl.program_id(1)

    @pl.when(i == 0)
    def _():
        carry_ref[...] = jnp.zeros_like(carry_ref)

    bd = bd_ref[...]
    q = qkv_ref[:, 0:FX_WIDTH].astype(F32)
    k = qkv_ref[:, FX_WIDTH:2 * FX_WIDTH].astype(F32)
    q = q * lax.rsqrt(_split_mm(q * q, bd) * (1.0 / HEAD_DIM) + EPS) * qg_ref[...]
    k = k * lax.rsqrt(_split_mm(k * k, bd) * (1.0 / HEAD_DIM) + EPS) * kg_ref[...]

    logf = -_softplus(-(small_ref[...] + fbias_ref[...]))
    c = _cumsum_rows(logf, tc) + carry_ref[...]
    carry_ref[...] = c[tc - 1:tc, :]

    c_hi = c.astype(BF16).astype(F32)
    r1 = c - c_hi
    c_mid = r1.astype(BF16).astype(F32)
    c_lo = (r1 - c_mid).astype(BF16).astype(F32)

    li = lax.broadcasted_iota(jnp.int32, (tc, HEAD_DIM), 1)
    for h in range(FX_HEADS):
        lane = LANE_F + h
        shape = (tc, HEAD_DIM)
        hi = jnp.broadcast_to(c_hi[:, lane:lane + 1], shape)
        mid = jnp.broadcast_to(c_mid[:, lane:lane + 1], shape)
        lo = jnp.broadcast_to(c_lo[:, lane:lane + 1], shape)
        q_ext = jnp.where(li == 0, hi, jnp.where(li == 1, mid, jnp.where(li == 2, lo,
                          jnp.where(li < 6, 1.0, 0.0))))
        k_ext = jnp.where(li < 3, 1.0, jnp.where(li == 3, -hi, jnp.where(li == 4, -mid,
                          jnp.where(li == 5, -lo, 0.0))))
        c0 = h * HEAD_DIM
        qa_ref[:, h * LANES:(h + 1) * LANES] = jnp.concatenate(
            [q[:, c0:c0 + HEAD_DIM], q_ext], axis=1).astype(BF16)
        ka_ref[:, h * LANES:(h + 1) * LANES] = jnp.concatenate(
            [k[:, c0:c0 + HEAD_DIM], k_ext], axis=1).astype(BF16)


def _fxprep(fx_qkv, small, fbias, qg, kg, bd, batch, seq, tc):
    n = batch * seq
    nt = seq // tc
    row = lambda w: pl.BlockSpec((tc, w), lambda b, i: (b * nt + i, 0))
    full = lambda a: pl.BlockSpec(a.shape, lambda b, i: (0,) * a.ndim)
    out = jax.ShapeDtypeStruct((n, FX_HEADS * LANES), BF16)
    return pl.pallas_call(
        functools.partial(_fxprep_kernel, tc=tc),
        grid=(batch, nt),
        in_specs=[row(QKV_W), row(LANES), full(fbias), full(qg), full(kg), full(bd)],
        out_specs=(row(FX_HEADS * LANES), row(FX_HEADS * LANES)),
        out_shape=(out, out),
        scratch_shapes=[pltpu.VMEM((1, LANES), F32)],
        compiler_params=pltpu.CompilerParams(dimension_semantics=("arbitrary", "arbitrary"),
                                             vmem_limit_bytes=VMEM_LIMIT),
        name="fxprep",
    )(fx_qkv, small, fbias, qg, kg, bd)


def _fxattn_kernel(qi_ref, kj_ref, qa_ref, ka_ref, v_ref, o_ref, m_ref, l_ref, acc_ref, *, tq):
    p = pl.program_id(2)
    i = qi_ref[p]
    j = kj_ref[p]

    @pl.when(j == 0)
    def _():
        m_ref[...] = jnp.full_like(m_ref, NEG_BIG)
        l_ref[...] = jnp.zeros_like(l_ref)
        acc_ref[...] = jnp.zeros_like(acc_ref)

    def step(masked):
        for hh in range(2):
            qa = qa_ref[:, hh * LANES:(hh + 1) * LANES]
            ka = ka_ref[:, hh * LANES:(hh + 1) * LANES]
            s = lax.dot_general(ka, qa, (((1,), (1,)), ((), ())), preferred_element_type=F32)
            if masked:
                row = lax.broadcasted_iota(jnp.int32, s.shape, 0)
                col = lax.broadcasted_iota(jnp.int32, s.shape, 1)
                s = jnp.where(row <= col, s, NEG_BIG)
            m_prev = m_ref[hh]
            m_new = jnp.maximum(m_prev, jnp.max(s, axis=0, keepdims=True))
            alpha = jnp.exp(m_prev - m_new)
            pexp = jnp.exp(s - m_new)
            l_ref[hh] = alpha * l_ref[hh] + jnp.sum(pexp, axis=0, keepdims=True)
            vh = v_ref[:, hh * HEAD_DIM:(hh + 1) * HEAD_DIM]
            pv = lax.dot_general(vh, pexp.astype(BF16), (((0,), (0,)), ((), ())),
                                 preferred_element_type=F32)
            acc_ref[hh] = alpha * acc_ref[hh] + pv
            m_ref[hh] = m_new

    @pl.when(j < i)
    def _():
        step(False)

    @pl.when(j == i)
    def _():
        step(True)
        outs = [(acc_ref[hh] / l_ref[hh]).T for hh in range(2)]
        o_ref[...] = jnp.concatenate(outs, axis=1).astype(BF16)


def _fxattn(qa, ka, fx_qkv, batch, seq, tq):
    n = batch * seq
    nq = seq // tq
    pairs = [(i, j) for i in range(nq) for j in range(i + 1)]
    qi = jnp.asarray(np.array([p[0] for p in pairs], np.int32))
    kj = jnp.asarray(np.array([p[1] for p in pairs], np.int32))
    v_col0 = (2 * FX_WIDTH) // LANES
    grid_spec = pltpu.PrefetchScalarGridSpec(
        num_scalar_prefetch=2,
        grid=(batch, FX_HEADS // 2, len(pairs)),
        in_specs=[
            pl.BlockSpec((tq, 2 * LANES), lambda b, hp, p, qi, kj: (b * nq + qi[p], hp)),
            pl.BlockSpec((tq, 2 * LANES), lambda b, hp, p, qi, kj: (b * nq + kj[p], hp)),
            pl.BlockSpec((tq, LANES), lambda b, hp, p, qi, kj: (b * nq + kj[p], v_col0 + hp)),
        ],
        out_specs=pl.BlockSpec((tq, LANES), lambda b, hp, p, qi, kj: (b * nq + qi[p], hp)),
        scratch_shapes=[
            pltpu.VMEM((2, 1, tq), F32),
            pltpu.VMEM((2, 1, tq), F32),
            pltpu.VMEM((2, HEAD_DIM, tq), F32),
        ],
    )
    return pl.pallas_call(
        functools.partial(_fxattn_kernel, tq=tq),
        grid_spec=grid_spec,
        out_shape=jax.ShapeDtypeStruct((n, FX_WIDTH), BF16),
        compiler_params=pltpu.CompilerParams(
            dimension_semantics=("arbitrary", "arbitrary", "arbitrary"),
            vmem_limit_bytes=VMEM_LIMIT),
        name="fxattn",
    )(qi, kj, qa, ka, fx_qkv)


def _outproj_kernel(x_ref, odn_ref, ofx_ref, px_ref, wout_ref, pbd_ref, pscale_ref, xo_ref, ext_ref, *, tc):
    i = pl.program_id(1)
    wmax = POOL_WINDOWS[-1]

    @pl.when(i == 0)
    def _():
        ext_ref[0:wmax, :] = jnp.zeros((wmax, POOL_WIDTH), F32)

    xp = px_ref[...].astype(F32)
    ext_ref[wmax:wmax + tc, :] = xp
    acc = xp
    sums = {}
    for j in range(1, wmax):
        acc = acc + ext_ref[pl.ds(wmax - j, tc), :]
        if j + 1 in POOL_WINDOWS:
            sums[j + 1] = acc
    ext_ref[0:wmax, :] = xp[tc - wmax:tc, :]

    grp = lax.broadcasted_iota(jnp.int32, (tc, POOL_WIDTH), 1) // POOL_GROUP_DIM
    t1 = lax.broadcasted_iota(jnp.int32, (tc, POOL_WIDTH), 0) + (i * tc + 1)
    wsum = sums[POOL_WINDOWS[-1]]
    wlen = jnp.full((tc, POOL_WIDTH), POOL_WINDOWS[-1], jnp.int32)
    for gi in range(POOL_GROUPS - 2, -1, -1):
        wsum = jnp.where(grp == gi, sums[POOL_WINDOWS[gi]], wsum)
        wlen = jnp.where(grp == gi, POOL_WINDOWS[gi], wlen)
    count = jnp.minimum(t1, wlen).astype(F32)
    y = wsum / count - xp
    pooled = _mm(y, pbd_ref[...]) * pscale_ref[...]

    out = x_ref[...]
    out = out + jnp.dot(odn_ref[...], wout_ref[0:DN_WIDTH, :], preferred_element_type=F32)
    out = out + jnp.dot(ofx_ref[...], wout_ref[DN_WIDTH:DN_WIDTH + FX_WIDTH, :], preferred_element_type=F32)
    out = out + jnp.dot(pooled.astype(BF16), wout_ref[DN_WIDTH + FX_WIDTH:D_MODEL, :],
                        preferred_element_type=F32)
    xo_ref[...] = out


def _outproj(xf, o_dn, o_fx, pool_x, w_out, pool_bd, pool_scale, batch, seq, tc):
    n = batch * seq
    nt = seq // tc
    row = lambda w: pl.BlockSpec((tc, w), lambda b, i: (b * nt + i, 0))
    full = lambda a: pl.BlockSpec(a.shape, lambda b, i: (0,) * a.ndim)
    return pl.pallas_call(
        functools.partial(_outproj_kernel, tc=tc),
        grid=(batch, nt),
        in_specs=[row(D_MODEL), row(DN_WIDTH), row(FX_WIDTH), row(POOL_WIDTH),
                  full(w_out), full(pool_bd), full(pool_scale)],
        out_specs=row(D_MODEL),
        out_shape=jax.ShapeDtypeStruct((n, D_MODEL), F32),
        scratch_shapes=[pltpu.VMEM((tc + POOL_WINDOWS[-1], POOL_WIDTH), F32)],
        compiler_params=pltpu.CompilerParams(dimension_semantics=("arbitrary", "arbitrary"),
                                             vmem_limit_bytes=VMEM_LIMIT),
        name="outproj",
    )(xf, o_dn, o_fx, pool_x, w_out, pool_bd, pool_scale)


def _ffn_kernel(x_ref, g_ref, wg_ref, wu_ref, wd_ref, o_ref, h_ref, acc_ref):
    f = pl.program_id(1)

    @pl.when(f == 0)
    def _():
        x = x_ref[...]
        ms = jnp.mean(x * x, axis=-1, keepdims=True)
        h_ref[...] = (x * lax.rsqrt(ms + EPS) * g_ref[...]).astype(BF16)
        acc_ref[...] = x

    h = h_ref[...]
    a = jnp.dot(h, wg_ref[...], preferred_element_type=F32)
    u = jnp.dot(h, wu_ref[...], preferred_element_type=F32)
    t = (a * _sigmoid(a) * u).astype(BF16)
    acc_ref[...] += jnp.dot(t, wd_ref[...], preferred_element_type=F32)

    @pl.when(f == pl.num_programs(1) - 1)
    def _():
        o_ref[...] = acc_ref[...]


def _ffn(xf, g, w_gate, w_up, w_down, tm, tf):
    n = xf.shape[0]
    nf = D_FF // tf
    return pl.pallas_call(
        _ffn_kernel,
        grid=(n // tm, nf),
        in_specs=[
            pl.BlockSpec((tm, D_MODEL), lambda i, f: (i, 0)),
            pl.BlockSpec((1, D_MODEL), lambda i, f: (0, 0)),
            pl.BlockSpec((D_MODEL, tf), lambda i, f: (0, f)),
            pl.BlockSpec((D_MODEL, tf), lambda i, f: (0, f)),
            pl.BlockSpec((tf, D_MODEL), lambda i, f: (f, 0)),
        ],
        out_specs=pl.BlockSpec((tm, D_MODEL), lambda i, f: (i, 0)),
        out_shape=jax.ShapeDtypeStruct((n, D_MODEL), F32),
        scratch_shapes=[pltpu.VMEM((tm, D_MODEL), BF16), pltpu.VMEM((tm, D_MODEL), F32)],
        compiler_params=pltpu.CompilerParams(dimension_semantics=("arbitrary", "arbitrary"),
                                             vmem_limit_bytes=VMEM_LIMIT),
        name="ffn",
    )(xf, g, w_gate, w_up, w_down)


def _moe_kernel(x_ref, g_ref, rhi_ref, rlo_ref, wg_ref, wu_ref, wd_ref, o_ref, h_ref, gate_ref, acc_ref):
    e = pl.program_id(1)
    tm = x_ref.shape[0]
    lane = lax.broadcasted_iota(jnp.int32, (tm, LANES), 1)

    @pl.when(e == 0)
    def _():
        x = x_ref[...]
        ms = jnp.mean(x * x, axis=-1, keepdims=True)
        hf = x * lax.rsqrt(ms + EPS) * g_ref[...]
        h_hi = hf.astype(BF16)
        h_lo = (hf - h_hi.astype(F32)).astype(BF16)
        h_ref[...] = h_hi
        acc_ref[...] = x
        logits = (jnp.dot(h_hi, rhi_ref[...], preferred_element_type=F32)
                  + jnp.dot(h_lo, rhi_ref[...], preferred_element_type=F32)
                  + jnp.dot(h_hi, rlo_ref[...], preferred_element_type=F32))
        logits = jnp.where(lane < N_EXPERTS, logits, NEG_BIG)
        ex = jnp.exp(logits - jnp.max(logits, axis=-1, keepdims=True))
        probs = ex / jnp.sum(ex, axis=-1, keepdims=True)
        p1 = jnp.max(probs, axis=-1, keepdims=True)
        i1 = jnp.min(jnp.where(probs == p1, lane, LANES), axis=-1, keepdims=True)
        rest = jnp.where(lane == i1, -1.0, probs)
        p2 = jnp.max(rest, axis=-1, keepdims=True)
        i2 = jnp.min(jnp.where(rest == p2, lane, LANES), axis=-1, keepdims=True)
        denom = p1 + p2
        gate_ref[...] = (jnp.where(lane == i1, p1 / denom, 0.0)
                         + jnp.where(lane == i2, p2 / denom, 0.0))

    h = h_ref[...]
    a = jnp.dot(h, wg_ref[...], preferred_element_type=F32)
    u = jnp.dot(h, wu_ref[...], preferred_element_type=F32)
    t = (a * _sigmoid(a) * u).astype(BF16)
    y = jnp.dot(t, wd_ref[...], preferred_element_type=F32)
    gate_e = jnp.sum(jnp.where(lane == e, gate_ref[...], 0.0), axis=-1, keepdims=True)
    acc_ref[...] += gate_e * y

    @pl.when(e == pl.num_programs(1) - 1)
    def _():
        o_ref[...] = acc_ref[...]


def _moe(xf, g, r_hi, r_lo, w_gate, w_up, w_down, tm):
    n = xf.shape[0]
    return pl.pallas_call(
        _moe_kernel,
        grid=(n // tm, N_EXPERTS),
        in_specs=[
            pl.BlockSpec((tm, D_MODEL), lambda i, e: (i, 0)),
            pl.BlockSpec((1, D_MODEL), lambda i, e: (0, 0)),
            pl.BlockSpec((D_MODEL, LANES), lambda i, e: (0, 0)),
            pl.BlockSpec((D_MODEL, LANES), lambda i, e: (0, 0)),
            pl.BlockSpec((None, D_MODEL, D_FF_EXPERT), lambda i, e: (e, 0, 0)),
            pl.BlockSpec((None, D_MODEL, D_FF_EXPERT), lambda i, e: (e, 0, 0)),
            pl.BlockSpec((None, D_FF_EXPERT, D_MODEL), lambda i, e: (e, 0, 0)),
        ],
        out_specs=pl.BlockSpec((tm, D_MODEL), lambda i, e: (i, 0)),
        out_shape=jax.ShapeDtypeStruct((n, D_MODEL), F32),
        scratch_shapes=[pltpu.VMEM((tm, D_MODEL), BF16), pltpu.VMEM((tm, LANES), F32),
                        pltpu.VMEM((tm, D_MODEL), F32)],
        compiler_params=pltpu.CompilerParams(dimension_semantics=("arbitrary", "arbitrary"),
                                             vmem_limit_bytes=VMEM_LIMIT),
        name="moe",
    )(xf, g, r_hi, r_lo, w_gate, w_up, w_down)


def _pad_lanes(v, offset):
    return jnp.zeros((1, LANES), F32).at[0, offset:offset + v.shape[0]].set(v.astype(F32))


def _block_diag_ones(width, group):
    idx = np.arange(width) // group
    return jnp.asarray((idx[:, None] == idx[None, :]).astype(np.float32), BF16)


def _tile(n, pref):
    return pref if n % pref == 0 else n


def _mixer(xf, batch, seq, norm1, w_in, dn_conv, dn_a_log, dn_dt_bias, dn_onorm, fx_qnorm, fx_knorm,
           fx_f_bias, pool_w, pool_scale, w_out):
    n = batch * seq
    o = 0
    sizes = (DN_WIDTH, DN_WIDTH, DN_WIDTH, DN_WIDTH, DN_HEADS, DN_HEADS,
             FX_WIDTH, FX_WIDTH, FX_WIDTH, FX_HEADS, POOL_WIDTH)
    cols = []
    for s in sizes:
        cols.append((o, o + s))
        o += s
    sl = lambda idx: w_in[:, cols[idx][0]:cols[idx][1]]
    w_big = jnp.concatenate([sl(0), sl(1), sl(2), sl(3), sl(6), sl(7), sl(8), sl(10)], axis=1).astype(BF16)
    w_small = jnp.concatenate(
        [sl(4), sl(5), sl(9), jnp.zeros((D_MODEL, LANES - 3 * DN_HEADS), F32)], axis=1).astype(BF16)

    dn_qkv, dn_z, fx_qkv, pool_x, small = _inproj(
        xf, norm1.reshape(1, D_MODEL), w_big, w_small, _tile(n, 512))

    bd = _block_diag_ones(DN_WIDTH, HEAD_DIM)
    gpar = jnp.concatenate([_pad_lanes(dn_a_log, LANE_A), _pad_lanes(dn_dt_bias, LANE_A)], axis=0)
    o_dn = _deltanet(dn_qkv, dn_z, small, dn_conv, gpar, jnp.tile(dn_onorm, DN_HEADS).reshape(1, DN_WIDTH),
                     bd, batch, seq, _tile(seq, 256))

    qg = (jnp.tile(fx_qnorm, FX_HEADS) * (HEAD_DIM ** -0.5)).reshape(1, FX_WIDTH)
    kg = jnp.tile(fx_knorm, FX_HEADS).reshape(1, FX_WIDTH)
    qa, ka = _fxprep(fx_qkv, small, _pad_lanes(fx_f_bias, LANE_F), qg, kg, bd, batch, seq, _tile(seq, 512))
    o_fx = _fxattn(qa, ka, fx_qkv, batch, seq, _tile(seq, 512))

    pool_bd = jax.scipy.linalg.block_diag(*[pool_w[gi] for gi in range(POOL_GROUPS)]).astype(BF16)
    return _outproj(xf, o_dn, o_fx, pool_x, w_out.astype(BF16), pool_bd,
                    pool_scale.reshape(1, POOL_WIDTH), batch, seq, _tile(seq, 512))


def kernel(x, norm1, w_in, dn_conv, dn_a_log, dn_dt_bias, dn_onorm, fx_qnorm, fx_knorm, fx_f_bias, pool_w,
           pool_scale, w_out, norm2, ffn_gate, ffn_up, ffn_down, router, moe_gate, moe_up, moe_down):
    batch, seq, _ = x.shape
    n = batch * seq
    xf = x.reshape(n, D_MODEL)
    depth = norm1.shape[0]
    for layer in range(depth):
        xf = _mixer(xf, batch, seq, norm1[layer], w_in[layer], dn_conv[layer], dn_a_log[layer],
                    dn_dt_bias[layer], dn_onorm[layer], fx_qnorm[layer], fx_knorm[layer],
                    fx_f_bias[layer], pool_w[layer], pool_scale[layer], w_out[layer])
        g2 = norm2[layer].reshape(1, D_MODEL)
        j = layer // 2
        if layer % 2 == 0:
            xf = _ffn(xf, g2, ffn_gate[j].astype(BF16), ffn_up[j].astype(BF16), ffn_down[j].astype(BF16),
                      _tile(n, 512), D_FF // 2)
        else:
            r = jnp.concatenate([router[j], jnp.zeros((D_MODEL, LANES - N_EXPERTS), F32)], axis=1)
            r_hi = r.astype(BF16)
            r_lo = (r - r_hi.astype(F32)).astype(BF16)
            xf = _moe(xf, g2, r_hi, r_lo, moe_gate[j].astype(BF16), moe_up[j].astype(BF16),
                      moe_down[j].astype(BF16), _tile(n, 1024))
    return xf.reshape(batch, seq, D_MODEL)
```

```python
import functools

import jax
import jax.numpy as jnp
import numpy as np
from jax import lax
from jax.experimental import pallas as pl
from jax.experimental.pallas import tpu as pltpu

F32 = jnp.float32
BF16 = jnp.bfloat16

D_MODEL = 1024
HEAD_DIM = 64
DN_HEADS = 6
DN_WIDTH = DN_HEADS * HEAD_DIM
FX_HEADS = 6
FX_WIDTH = FX_HEADS * HEAD_DIM
POOL_GROUPS = 4
POOL_GROUP_DIM = 64
POOL_WIDTH = POOL_GROUPS * POOL_GROUP_DIM
POOL_WINDOWS = (2, 4, 8, 16)
CONV_WIDTH = 4
DN_CHUNK = 64
D_FF = 2816
N_EXPERTS = 8
D_FF_EXPERT = 1536
EPS = 1e-6
LANES = 128
NEG_BIG = -1e30
LOG2E = 1.4426950408889634

QKV_W = 3 * DN_WIDTH
COL_DN_QKV = 0
COL_DN_Z = COL_DN_QKV + QKV_W
COL_FX_QKV = COL_DN_Z + DN_WIDTH
COL_POOL = COL_FX_QKV + QKV_W
N_BIG = COL_POOL + POOL_WIDTH
LANE_A = 0
LANE_B = DN_HEADS
LANE_F = 2 * DN_HEADS

VMEM_LIMIT = 56 * 1024 * 1024


def _mm(a, b):
    return jnp.dot(a.astype(BF16), b.astype(BF16), preferred_element_type=F32)


def _mm_nt(a, b):
    return lax.dot_general(a.astype(BF16), b.astype(BF16), (((1,), (1,)), ((), ())),
                           preferred_element_type=F32)


def _mm_tn(a, b):
    return lax.dot_general(a.astype(BF16), b.astype(BF16), (((0,), (0,)), ((), ())),
                           preferred_element_type=F32)


def _split_mm(a, b_bf16):
    hi = a.astype(BF16)
    lo = (a - hi.astype(F32)).astype(BF16)
    return (jnp.dot(hi, b_bf16, preferred_element_type=F32)
            + jnp.dot(lo, b_bf16, preferred_element_type=F32))


def _sigmoid(x):
    return 1.0 / (1.0 + jnp.exp(-x))


def _softplus(x):
    return jnp.maximum(x, 0.0) + jnp.log(1.0 + jnp.exp(-jnp.abs(x)))


def _cumsum_rows(x, period):
    ridx = lax.broadcasted_iota(jnp.int32, x.shape, 0) & (period - 1)
    s = 1
    while s < period:
        x = x + jnp.where(ridx >= s, pltpu.roll(x, s, axis=0), 0.0)
        s *= 2
    return x


SRC_AB = 4 * DN_WIDTH
SRC_FX_QKV = SRC_AB + 2 * DN_HEADS
SRC_F = SRC_FX_QKV + QKV_W
SRC_POOL = SRC_F + FX_HEADS
N_IN = SRC_POOL + POOL_WIDTH


def _winprep_kernel(w_ref, big_ref, small_ref):
    big_ref[:, COL_DN_QKV:COL_FX_QKV] = w_ref[:, 0:SRC_AB].astype(BF16)
    big_ref[:, COL_FX_QKV:COL_POOL] = w_ref[:, SRC_FX_QKV:SRC_F].astype(BF16)
    big_ref[:, COL_POOL:N_BIG] = w_ref[:, SRC_POOL:N_IN].astype(BF16)
    lane = lax.broadcasted_iota(jnp.int32, small_ref.shape, 1)
    ab = w_ref[:, SRC_AB:SRC_AB + LANES]
    ff = w_ref[:, SRC_F - LANE_F:SRC_F - LANE_F + LANES]
    small_ref[...] = jnp.where(lane < LANE_F, ab, jnp.where(lane < LANE_F + FX_HEADS, ff, 0.0)).astype(BF16)


def _winprep(w_in):
    assert w_in.shape == (D_MODEL, N_IN) and (SRC_F - LANE_F) % LANES == 0
    rows = 256
    return pl.pallas_call(
        _winprep_kernel,
        grid=(D_MODEL // rows,),
        in_specs=[pl.BlockSpec((rows, N_IN), lambda i: (i, 0))],
        out_specs=(pl.BlockSpec((rows, N_BIG), lambda i: (i, 0)),
                   pl.BlockSpec((rows, LANES), lambda i: (i, 0))),
        out_shape=(jax.ShapeDtypeStruct((D_MODEL, N_BIG), BF16),
                   jax.ShapeDtypeStruct((D_MODEL, LANES), BF16)),
        compiler_params=pltpu.CompilerParams(dimension_semantics=("arbitrary",),
                                             vmem_limit_bytes=VMEM_LIMIT),
        name="winprep",
    )(w_in)


def _inproj_kernel(x_ref, g_ref, w_ref, ws_ref, dnqkv_ref, dnz_ref, fxqkv_ref, pool_ref, small_ref):
    x = x_ref[...]
    ms = jnp.mean(x * x, axis=-1, keepdims=True)
    h = (x * lax.rsqrt(ms + EPS) * g_ref[...]).astype(BF16)

    def proj(lo, hi):
        return jnp.dot(h, w_ref[:, lo:hi], preferred_element_type=F32)

    dnqkv_ref[...] = proj(COL_DN_QKV, COL_DN_Z).astype(BF16)
    dnz_ref[...] = proj(COL_DN_Z, COL_FX_QKV).astype(BF16)
    fxqkv_ref[...] = proj(COL_FX_QKV, COL_POOL).astype(BF16)
    pool_ref[...] = proj(COL_POOL, N_BIG).astype(BF16)
    small_ref[...] = jnp.dot(h, ws_ref[...], preferred_element_type=F32)


def _inproj(xf, g, w_big, w_small, tm):
    n = xf.shape[0]
    out_shape = (
        jax.ShapeDtypeStruct((n, QKV_W), BF16),
        jax.ShapeDtypeStruct((n, DN_WIDTH), BF16),
        jax.ShapeDtypeStruct((n, QKV_W), BF16),
        jax.ShapeDtypeStruct((n, POOL_WIDTH), BF16),
        jax.ShapeDtypeStruct((n, LANES), F32),
    )
    row = lambda w: pl.BlockSpec((tm, w), lambda i: (i, 0))
    full = lambda a: pl.BlockSpec(a.shape, lambda i: (0,) * a.ndim)
    return pl.pallas_call(
        _inproj_kernel,
        grid=(n // tm,),
        in_specs=[row(D_MODEL), full(g), full(w_big), full(w_small)],
        out_specs=(row(QKV_W), row(DN_WIDTH), row(QKV_W), row(POOL_WIDTH), row(LANES)),
        out_shape=out_shape,
        compiler_params=pltpu.CompilerParams(dimension_semantics=("arbitrary",),
                                             vmem_limit_bytes=VMEM_LIMIT),
        name="inproj",
    )(xf, g, w_big, w_small)


def _dotf(a, b):
    return jnp.dot(a, b, preferred_element_type=F32)


def _unit_lower_inverses(a_list, masks_ref, eye):
    m8 = masks_ref[0]
    d = [a * m8 for a in a_list]
    db = [x.astype(BF16) for x in d]
    d2 = [_dotf(x, x).astype(BF16) for x in db]
    d4 = [_dotf(x, x).astype(BF16) for x in d2]
    x = [eye - dd for dd in d]
    x = [xx + _dotf(xx.astype(BF16), y) for xx, y in zip(x, d2)]
    x = [xx + _dotf(xx.astype(BF16), y) for xx, y in zip(x, d4)]
    for level in (1, 2, 3):
        me = masks_ref[level]
        e = [(a * me).astype(BF16) for a in a_list]
        xb = [xx.astype(BF16) for xx in x]
        ex = [_dotf(ee, xx).astype(BF16) for ee, xx in zip(e, xb)]
        x = [xx - _dotf(xxb, eex) for xx, xxb, eex in zip(x, xb, ex)]
    return x


def _dn_kernel(qkv_ref, z_ref, small_ref, convw_ref, gpar_ref, onorm_ref, bd_ref, ea_ref, eb_ref, masks_ref,
               o_ref, ext_ref, s_ref, oscr_ref, *, tc):
    i = pl.program_id(1)
    n_chunks = tc // DN_CHUNK
    heads = range(DN_HEADS)

    @pl.when(i == 0)
    def _():
        ext_ref[0:8, :] = jnp.zeros((8, QKV_W), F32)
        s_ref[...] = jnp.zeros_like(s_ref)

    x = qkv_ref[...].astype(F32)
    ext_ref[8:8 + tc, :] = x
    w = convw_ref[...]
    y = x * w[CONV_WIDTH - 1:CONV_WIDTH, :]
    for j in range(CONV_WIDTH - 1):
        y = y + ext_ref[pl.ds(8 - (CONV_WIDTH - 1) + j, tc), :] * w[j:j + 1, :]
    ext_ref[0:8, :] = x[tc - 8:tc, :]
    y = y * _sigmoid(y)

    bd = bd_ref[...]
    q = y[:, 0:DN_WIDTH]
    k = y[:, DN_WIDTH:2 * DN_WIDTH]
    v = y[:, 2 * DN_WIDTH:3 * DN_WIDTH]
    q = q * lax.rsqrt(_split_mm(q * q, bd) + EPS) * (HEAD_DIM ** -0.5)
    k = k * lax.rsqrt(_split_mm(k * k, bd) + EPS)

    sm = small_ref[...]
    gp = gpar_ref[...]
    g = -jnp.exp(gp[0:1, :]) * _softplus(sm + gp[1:2, :])
    beta = _sigmoid(sm)
    gcum = _cumsum_rows(g, DN_CHUNK)
    chunk_of_row = lax.broadcasted_iota(jnp.int32, (tc, LANES), 0) // DN_CHUNK
    glast = jnp.broadcast_to(gcum[tc - 1:tc, :], (tc, LANES))
    for c in range(n_chunks - 2, -1, -1):
        glast = jnp.where(chunk_of_row == c, gcum[(c + 1) * DN_CHUNK - 1:(c + 1) * DN_CHUNK, :], glast)

    ea = ea_ref[...]
    eb = eb_ref[...]
    beta_w = _split_mm(beta, eb)
    eg_w = _split_mm(jnp.exp(gcum), ea)
    ekd_w = _split_mm(jnp.exp(glast - gcum), ea)
    egl_w = _split_mm(jnp.exp(glast), ea)
    kb = k * beta_w
    vb = v * beta_w
    kbg = kb * eg_w
    q_dec = q * eg_w
    k_dec = k * ekd_w
    gcum_t = gcum.T

    def hs(a, h):
        return a[:, h * HEAD_DIM:(h + 1) * HEAD_DIM]

    r_i = lax.broadcasted_iota(jnp.int32, (tc, tc), 0)
    c_i = lax.broadcasted_iota(jnp.int32, (tc, tc), 1)
    same_chunk = (r_i // DN_CHUNK) == (c_i // DN_CHUNK)
    causal = jnp.logical_and(same_chunk, r_i >= c_i)
    eye = jnp.where(r_i == c_i, 1.0, 0.0)

    decay = []
    for h in heads:
        diff = gcum[:, LANE_A + h:LANE_A + h + 1] - gcum_t[LANE_A + h:LANE_A + h + 1, :]
        decay.append(jnp.where(causal, jnp.exp(jnp.where(causal, diff, 0.0)), 0.0))
    prod = [_mm_nt(jnp.concatenate([hs(q, h), hs(kb, h)], axis=0), hs(k, h)) for h in heads]
    qk = [(prod[h][0:tc] * decay[h]).astype(BF16) for h in heads]
    a_mat = [prod[h][tc:2 * tc] * decay[h] * (1.0 - eye) for h in heads]
    t_inv = _unit_lower_inverses(a_mat, masks_ref, eye)
    uw = [_dotf(t_inv[h].astype(BF16),
                jnp.concatenate([hs(vb, h), hs(kbg, h)], axis=1).astype(BF16)).astype(BF16) for h in heads]
    k_bd = [jnp.where(same_chunk, jnp.concatenate([hs(k_dec, h)] * n_chunks, axis=1), 0.0).astype(BF16)
            for h in heads]
    ktuw = [lax.dot_general(k_bd[h], uw[h], (((0,), (0,)), ((), ())), preferred_element_type=F32)
            for h in heads]
    qkuw = [_dotf(qk[h], uw[h]) for h in heads]

    for c in range(n_chunks):
        r0 = c * DN_CHUNK
        r1 = r0 + DN_CHUNK
        s_all = s_ref[...]
        lhs = [jnp.concatenate([hs(q_dec, h)[r0:r1] - qkuw[h][r0:r1, HEAD_DIM:2 * HEAD_DIM],
                                ktuw[h][r0:r1, HEAD_DIM:2 * HEAD_DIM]], axis=0) for h in heads]
        res = [_mm(lhs[h], hs(s_all, h)) for h in heads]
        oscr_ref[r0:r1, :] = jnp.concatenate(
            [res[h][0:DN_CHUNK] + qkuw[h][r0:r1, 0:HEAD_DIM] for h in heads], axis=1)
        s_ref[...] = s_all * egl_w[r0:r0 + 1, :] + jnp.concatenate(
            [ktuw[h][r0:r1, 0:HEAD_DIM] - res[h][DN_CHUNK:2 * DN_CHUNK] for h in heads], axis=1)

    o = oscr_ref[...]
    ms = _split_mm(o * o, bd) * (1.0 / HEAD_DIM)
    o = o * lax.rsqrt(ms + EPS) * onorm_ref[...]
    zz = z_ref[...].astype(F32)
    o_ref[...] = (o * (zz * _sigmoid(zz))).astype(BF16)


def _dn_constants(tc):
    ea = np.zeros((LANES, DN_WIDTH), np.float32)
    eb = np.zeros((LANES, DN_WIDTH), np.float32)
    for h in range(DN_HEADS):
        ea[LANE_A + h, h * HEAD_DIM:(h + 1) * HEAD_DIM] = 1.0
        eb[LANE_B + h, h * HEAD_DIM:(h + 1) * HEAD_DIM] = 1.0
    r = np.arange(tc)[:, None]
    c = np.arange(tc)[None, :]
    masks = [(r >> 3) == (c >> 3)]
    for s in (3, 4, 5):
        masks.append(((r >> s) ^ (c >> s)) == 1)
    masks = np.stack(masks).astype(np.float32)
    return jnp.asarray(ea, BF16), jnp.asarray(eb, BF16), jnp.asarray(masks, F32)


def _deltanet(dn_qkv, dn_z, small, conv_w, gpar, onorm, bd, batch, seq, tc):
    n = batch * seq
    nt = seq // tc
    ea, eb, masks = _dn_constants(tc)
    row = lambda w: pl.BlockSpec((tc, w), lambda b, i: (b * nt + i, 0))
    full = lambda a: pl.BlockSpec(a.shape, lambda b, i: (0,) * a.ndim)
    return pl.pallas_call(
        functools.partial(_dn_kernel, tc=tc),
        grid=(batch, nt),
        in_specs=[row(QKV_W), row(DN_WIDTH), row(LANES), full(conv_w), full(gpar), full(onorm), full(bd),
                  full(ea), full(eb), full(masks)],
        out_specs=row(DN_WIDTH),
        out_shape=jax.ShapeDtypeStruct((n, DN_WIDTH), BF16),
        scratch_shapes=[
            pltpu.VMEM((tc + 8, QKV_W), F32),
            pltpu.VMEM((HEAD_DIM, DN_WIDTH), F32),
            pltpu.VMEM((tc, DN_WIDTH), F32),
        ],
        compiler_params=pltpu.CompilerParams(dimension_semantics=("arbitrary", "arbitrary"),
                                             vmem_limit_bytes=VMEM_LIMIT),
        name="deltanet",
    )(dn_qkv, dn_z, small, conv_w, gpar, onorm, bd, ea, eb, masks)


def _fxprep_kernel(qkv_ref, small_ref, fbias_ref, qg_ref, kg_ref, bd_ref, qa_ref, ka_ref, carry_ref, *, tc):
    i = pl.program_id(1)

    @pl.when(i == 0)
    def _():
        carry_ref[...] = jnp.zeros_like(carry_ref)

    bd = bd_ref[...]
    q = qkv_ref[:, 0:FX_WIDTH].astype(F32)
    k = qkv_ref[:, FX_WIDTH:2 * FX_WIDTH].astype(F32)
    q = q * lax.rsqrt(_split_mm(q * q, bd) * (1.0 / HEAD_DIM) + EPS) * qg_ref[...]
    k = k * lax.rsqrt(_split_mm(k * k, bd) * (1.0 / HEAD_DIM) + EPS) * kg_ref[...]

    logf = -_softplus(-(small_ref[...] + fbias_ref[...]))
    c = _cumsum_rows(logf, tc) + carry_ref[...]
    carry_ref[...] = c[tc - 1:tc, :]
    c = c * LOG2E

    c_hi = c.astype(BF16).astype(F32)
    r1 = c - c_hi
    c_mid = r1.astype(BF16).astype(F32)
    c_lo = (r1 - c_mid).astype(BF16).astype(F32)

    li = lax.broadcasted_iota(jnp.int32, (tc, HEAD_DIM), 1)
    for h in range(FX_HEADS):
        lane = LANE_F + h
        shape = (tc, HEAD_DIM)
        hi = jnp.broadcast_to(c_hi[:, lane:lane + 1], shape)
        mid = jnp.broadcast_to(c_mid[:, lane:lane + 1], shape)
        lo = jnp.broadcast_to(c_lo[:, lane:lane + 1], shape)
        q_ext = jnp.where(li == 0, hi, jnp.where(li == 1, mid, jnp.where(li == 2, lo,
                          jnp.where(li < 6, 1.0, 0.0))))
        k_ext = jnp.where(li < 3, 1.0, jnp.where(li == 3, -hi, jnp.where(li == 4, -mid,
                          jnp.where(li == 5, -lo, 0.0))))
        c0 = h * HEAD_DIM
        qa_ref[:, h * LANES:(h + 1) * LANES] = jnp.concatenate(
            [q[:, c0:c0 + HEAD_DIM], q_ext], axis=1).astype(BF16)
        ka_ref[:, h * LANES:(h + 1) * LANES] = jnp.concatenate(
            [k[:, c0:c0 + HEAD_DIM], k_ext], axis=1).astype(BF16)


def _fxprep(fx_qkv, small, fbias, qg, kg, bd, batch, seq, tc):
    n = batch * seq
    nt = seq // tc
    row = lambda w: pl.BlockSpec((tc, w), lambda b, i: (b * nt + i, 0))
    full = lambda a: pl.BlockSpec(a.shape, lambda b, i: (0,) * a.ndim)
    out = jax.ShapeDtypeStruct((n, FX_HEADS * LANES), BF16)
    return pl.pallas_call(
        functools.partial(_fxprep_kernel, tc=tc),
        grid=(batch, nt),
        in_specs=[row(QKV_W), row(LANES), full(fbias), full(qg), full(kg), full(bd)],
        out_specs=(row(FX_HEADS * LANES), row(FX_HEADS * LANES)),
        out_shape=(out, out),
        scratch_shapes=[pltpu.VMEM((1, LANES), F32)],
        compiler_params=pltpu.CompilerParams(dimension_semantics=("arbitrary", "arbitrary"),
                                             vmem_limit_bytes=VMEM_LIMIT),
        name="fxprep",
    )(fx_qkv, small, fbias, qg, kg, bd)


def _fxattn_kernel(qa_ref, ka_ref, v_ref, o_ref, *, seq, tq, tk, nsplit):
    w = tq // nsplit
    chains = [(hh, qs) for hh in range(2) for qs in range(nsplit)]
    blocks = [(qi, j) for qi in range(seq // tq) for j in range((qi + 1) * (tq // tk))]
    last_of_tile = {qi: (qi + 1) * (tq // tk) - 1 for qi in range(seq // tq)}
    state = {}
    scores = {}
    probs = {}

    def stage_scores(g):
        qi, j = blocks[g]
        q0, k0 = qi * tq, j * tk
        out = []
        for hh, qs in chains:
            qa = qa_ref[q0 + qs * w:q0 + (qs + 1) * w, hh * LANES:(hh + 1) * LANES]
            ka = ka_ref[k0:k0 + tk, hh * LANES:(hh + 1) * LANES]
            sc = lax.dot_general(ka, qa, (((1,), (1,)), ((), ())), preferred_element_type=F32)
            first_q = q0 + qs * w
            if k0 + tk - 1 > first_q:
                row = lax.broadcasted_iota(jnp.int32, sc.shape, 0)
                col = lax.broadcasted_iota(jnp.int32, sc.shape, 1)
                sc = jnp.where(row + (k0 - first_q) <= col, sc, NEG_BIG)
            out.append(sc)
        scores[g] = out

    def stage_softmax(g):
        qi, j = blocks[g]
        if j == 0:
            state[qi] = [(jnp.full((1, w), NEG_BIG, F32), jnp.zeros((1, w), F32),
                          jnp.zeros((HEAD_DIM, w), F32)) for _ in chains]
        out = []
        for ci in range(len(chains)):
            m_prev, l_prev, acc = state[qi][ci]
            sc = scores[g][ci]
            m_new = jnp.maximum(m_prev, jnp.max(sc, axis=0, keepdims=True))
            a = jnp.exp2(m_prev - m_new)
            p = jnp.exp2(sc - m_new)
            state[qi][ci] = (m_new, a * l_prev + jnp.sum(p, axis=0, keepdims=True), acc)
            out.append((p.astype(BF16), a))
        del scores[g]
        probs[g] = out

    def stage_values(g):
        qi, j = blocks[g]
        k0 = j * tk
        for ci, (hh, qs) in enumerate(chains):
            p, a = probs[g][ci]
            m_cur, l_cur, acc = state[qi][ci]
            vh = v_ref[k0:k0 + tk, hh * HEAD_DIM:(hh + 1) * HEAD_DIM]
            pv = lax.dot_general(vh, p, (((0,), (0,)), ((), ())), preferred_element_type=F32)
            state[qi][ci] = (m_cur, l_cur, a * acc + pv)
        del probs[g]
        if j == last_of_tile[qi]:
            cols = []
            for hh in range(2):
                per_q = [state[qi][hh * nsplit + qs] for qs in range(nsplit)]
                o_t = jnp.concatenate([acc / l_cur for (_, l_cur, acc) in per_q], axis=1)
                cols.append(o_t.T)
            o_ref[qi * tq:(qi + 1) * tq, :] = jnp.concatenate(cols, axis=1).astype(BF16)
            del state[qi]

    n_blocks = len(blocks)
    for t in range(n_blocks + 2):
        if t < n_blocks:
            stage_scores(t)
        if 0 <= t - 1 < n_blocks:
            stage_softmax(t - 1)
        if 0 <= t - 2 < n_blocks:
            stage_values(t - 2)


def _fxattn(qa, ka, fx_qkv, batch, seq, tq, tk, nsplit):
    n = batch * seq
    v_col0 = (2 * FX_WIDTH) // LANES
    return pl.pallas_call(
        functools.partial(_fxattn_kernel, seq=seq, tq=tq, tk=tk, nsplit=nsplit),
        grid=(batch, FX_HEADS // 2),
        in_specs=[
            pl.BlockSpec((seq, 2 * LANES), lambda b, hp: (b, hp)),
            pl.BlockSpec((seq, 2 * LANES), lambda b, hp: (b, hp)),
            pl.BlockSpec((seq, LANES), lambda b, hp: (b, v_col0 + hp)),
        ],
        out_specs=pl.BlockSpec((seq, LANES), lambda b, hp: (b, hp)),
        out_shape=jax.ShapeDtypeStruct((n, FX_WIDTH), BF16),
        compiler_params=pltpu.CompilerParams(dimension_semantics=("arbitrary", "arbitrary"),
                                             vmem_limit_bytes=VMEM_LIMIT),
        name="fxattn",
    )(qa, ka, fx_qkv)


def _outproj_kernel(x_ref, odn_ref, ofx_ref, px_ref, wout_ref, pbd_ref, pscale_ref, xo_ref, ext_ref, *, tc):
    i = pl.program_id(1)
    wmax = POOL_WINDOWS[-1]

    @pl.when(i == 0)
    def _():
        ext_ref[0:wmax, :] = jnp.zeros((wmax, POOL_WIDTH), F32)

    xp = px_ref[...].astype(F32)
    ext_ref[wmax:wmax + tc, :] = xp
    acc = xp
    sums = {}
    for j in range(1, wmax):
        acc = acc + ext_ref[pl.ds(wmax - j, tc), :]
        if j + 1 in POOL_WINDOWS:
            sums[j + 1] = acc
    ext_ref[0:wmax, :] = xp[tc - wmax:tc, :]

    grp = lax.broadcasted_iota(jnp.int32, (tc, POOL_WIDTH), 1) // POOL_GROUP_DIM
    t1 = lax.broadcasted_iota(jnp.int32, (tc, POOL_WIDTH), 0) + (i * tc + 1)
    wsum = sums[POOL_WINDOWS[-1]]
    wlen = jnp.full((tc, POOL_WIDTH), POOL_WINDOWS[-1], jnp.int32)
    for gi in range(POOL_GROUPS - 2, -1, -1):
        wsum = jnp.where(grp == gi, sums[POOL_WINDOWS[gi]], wsum)
        wlen = jnp.where(grp == gi, POOL_WINDOWS[gi], wlen)
    count = jnp.minimum(t1, wlen).astype(F32)
    y = wsum / count - xp
    pooled = _mm(y, pbd_ref[...]) * pscale_ref[...]

    out = x_ref[...]
    out = out + jnp.dot(odn_ref[...], wout_ref[0:DN_WIDTH, :], preferred_element_type=F32)
    out = out + jnp.dot(ofx_ref[...], wout_ref[DN_WIDTH:DN_WIDTH + FX_WIDTH, :], preferred_element_type=F32)
    out = out + jnp.dot(pooled.astype(BF16), wout_ref[DN_WIDTH + FX_WIDTH:D_MODEL, :],
                        preferred_element_type=F32)
    xo_ref[...] = out


def _outproj(xf, o_dn, o_fx, pool_x, w_out, pool_bd, pool_scale, batch, seq, tc):
    n = batch * seq
    nt = seq // tc
    row = lambda w: pl.BlockSpec((tc, w), lambda b, i: (b * nt + i, 0))
    full = lambda a: pl.BlockSpec(a.shape, lambda b, i: (0,) * a.ndim)
    return pl.pallas_call(
        functools.partial(_outproj_kernel, tc=tc),
        grid=(batch, nt),
        in_specs=[row(D_MODEL), row(DN_WIDTH), row(FX_WIDTH), row(POOL_WIDTH),
                  full(w_out), full(pool_bd), full(pool_scale)],
        out_specs=row(D_MODEL),
        out_shape=jax.ShapeDtypeStruct((n, D_MODEL), F32),
        scratch_shapes=[pltpu.VMEM((tc + POOL_WINDOWS[-1], POOL_WIDTH), F32)],
        compiler_params=pltpu.CompilerParams(dimension_semantics=("arbitrary", "arbitrary"),
                                             vmem_limit_bytes=VMEM_LIMIT),
        name="outproj",
    )(xf, o_dn, o_fx, pool_x, w_out, pool_bd, pool_scale)


def _ffn_kernel(x_ref, g_ref, wg_ref, wu_ref, wd_ref, o_ref, h_ref, acc_ref):
    f = pl.program_id(1)

    @pl.when(f == 0)
    def _():
        x = x_ref[...]
        ms = jnp.mean(x * x, axis=-1, keepdims=True)
        h_ref[...] = (x * lax.rsqrt(ms + EPS) * g_ref[...]).astype(BF16)
        acc_ref[...] = x

    h = h_ref[...]
    a = jnp.dot(h, wg_ref[...], preferred_element_type=F32)
    u = jnp.dot(h, wu_ref[...], preferred_element_type=F32)
    t = (a * _sigmoid(a) * u).astype(BF16)
    acc_ref[...] += jnp.dot(t, wd_ref[...], preferred_element_type=F32)

    @pl.when(f == pl.num_programs(1) - 1)
    def _():
        o_ref[...] = acc_ref[...]


def _ffn(xf, g, w_gate, w_up, w_down, tm, tf):
    n = xf.shape[0]
    nf = D_FF // tf
    return pl.pallas_call(
        _ffn_kernel,
        grid=(n // tm, nf),
        in_specs=[
            pl.BlockSpec((tm, D_MODEL), lambda i, f: (i, 0)),
            pl.BlockSpec((1, D_MODEL), lambda i, f: (0, 0)),
            pl.BlockSpec((D_MODEL, tf), lambda i, f: (0, f)),
            pl.BlockSpec((D_MODEL, tf), lambda i, f: (0, f)),
            pl.BlockSpec((tf, D_MODEL), lambda i, f: (f, 0)),
        ],
        out_specs=pl.BlockSpec((tm, D_MODEL), lambda i, f: (i, 0)),
        out_shape=jax.ShapeDtypeStruct((n, D_MODEL), F32),
        scratch_shapes=[pltpu.VMEM((tm, D_MODEL), BF16), pltpu.VMEM((tm, D_MODEL), F32)],
        compiler_params=pltpu.CompilerParams(dimension_semantics=("arbitrary", "arbitrary"),
                                             vmem_limit_bytes=VMEM_LIMIT),
        name="ffn",
    )(xf, g, w_gate, w_up, w_down)


def _moe_kernel(x_ref, g_ref, rhi_ref, rlo_ref, wg_ref, wu_ref, wd_ref, o_ref, h_ref, gate_ref, acc_ref):
    e = pl.program_id(1)
    tm = x_ref.shape[0]
    lane = lax.broadcasted_iota(jnp.int32, (tm, LANES), 1)

    @pl.when(e == 0)
    def _():
        x = x_ref[...]
        ms = jnp.mean(x * x, axis=-1, keepdims=True)
        hf = x * lax.rsqrt(ms + EPS) * g_ref[...]
        h_hi = hf.astype(BF16)
        h_lo = (hf - h_hi.astype(F32)).astype(BF16)
        h_ref[...] = h_hi
        acc_ref[...] = x
        logits = (jnp.dot(h_hi, rhi_ref[...], preferred_element_type=F32)
                  + jnp.dot(h_lo, rhi_ref[...], preferred_element_type=F32)
                  + jnp.dot(h_hi, rlo_ref[...], preferred_element_type=F32))
        logits = jnp.where(lane < N_EXPERTS, logits, NEG_BIG)
        ex = jnp.exp(logits - jnp.max(logits, axis=-1, keepdims=True))
        probs = ex / jnp.sum(ex, axis=-1, keepdims=True)
        p1 = jnp.max(probs, axis=-1, keepdims=True)
        i1 = jnp.min(jnp.where(probs == p1, lane, LANES), axis=-1, keepdims=True)
        rest = jnp.where(lane == i1, -1.0, probs)
        p2 = jnp.max(rest, axis=-1, keepdims=True)
        i2 = jnp.min(jnp.where(rest == p2, lane, LANES), axis=-1, keepdims=True)
        denom = p1 + p2
        gate_ref[...] = (jnp.where(lane == i1, p1 / denom, 0.0)
                         + jnp.where(lane == i2, p2 / denom, 0.0))

    h = h_ref[...]
    a = jnp.dot(h, wg_ref[...], preferred_element_type=F32)
    u = jnp.dot(h, wu_ref[...], preferred_element_type=F32)
    t = (a * _sigmoid(a) * u).astype(BF16)
    y = jnp.dot(t, wd_ref[...], preferred_element_type=F32)
    gate_e = jnp.sum(jnp.where(lane == e, gate_ref[...], 0.0), axis=-1, keepdims=True)
    acc_ref[...] += gate_e * y

    @pl.when(e == pl.num_programs(1) - 1)
    def _():
        o_ref[...] = acc_ref[...]


def _moe(xf, g, r_hi, r_lo, w_gate, w_up, w_down, tm):
    n = xf.shape[0]
    return pl.pallas_call(
        _moe_kernel,
        grid=(n // tm, N_EXPERTS),
        in_specs=[
            pl.BlockSpec((tm, D_MODEL), lambda i, e: (i, 0)),
            pl.BlockSpec((1, D_MODEL), lambda i, e: (0, 0)),
            pl.BlockSpec((D_MODEL, LANES), lambda i, e: (0, 0)),
            pl.BlockSpec((D_MODEL, LANES), lambda i, e: (0, 0)),
            pl.BlockSpec((None, D_MODEL, D_FF_EXPERT), lambda i, e: (e, 0, 0)),
            pl.BlockSpec((None, D_MODEL, D_FF_EXPERT), lambda i, e: (e, 0, 0)),
            pl.BlockSpec((None, D_FF_EXPERT, D_MODEL), lambda i, e: (e, 0, 0)),
        ],
        out_specs=pl.BlockSpec((tm, D_MODEL), lambda i, e: (i, 0)),
        out_shape=jax.ShapeDtypeStruct((n, D_MODEL), F32),
        scratch_shapes=[pltpu.VMEM((tm, D_MODEL), BF16), pltpu.VMEM((tm, LANES), F32),
                        pltpu.VMEM((tm, D_MODEL), F32)],
        compiler_params=pltpu.CompilerParams(dimension_semantics=("arbitrary", "arbitrary"),
                                             vmem_limit_bytes=VMEM_LIMIT),
        name="moe",
    )(xf, g, r_hi, r_lo, w_gate, w_up, w_down)


def _pad_lanes(v, offset):
    return jnp.zeros((1, LANES), F32).at[0, offset:offset + v.shape[0]].set(v.astype(F32))


def _block_diag_ones(width, group):
    idx = np.arange(width) // group
    return jnp.asarray((idx[:, None] == idx[None, :]).astype(np.float32), BF16)


def _tile(n, pref):
    return pref if n % pref == 0 else n


def _mixer(xf, batch, seq, norm1, w_in, dn_conv, dn_a_log, dn_dt_bias, dn_onorm, fx_qnorm, fx_knorm,
           fx_f_bias, pool_w, pool_scale, w_out):
    n = batch * seq
    w_big, w_small = _winprep(w_in)

    dn_qkv, dn_z, fx_qkv, pool_x, small = _inproj(
        xf, norm1.reshape(1, D_MODEL), w_big, w_small, _tile(n, 512))

    bd = _block_diag_ones(DN_WIDTH, HEAD_DIM)
    gpar = jnp.concatenate([_pad_lanes(dn_a_log, LANE_A), _pad_lanes(dn_dt_bias, LANE_A)], axis=0)
    o_dn = _deltanet(dn_qkv, dn_z, small, dn_conv, gpar, jnp.tile(dn_onorm, DN_HEADS).reshape(1, DN_WIDTH),
                     bd, batch, seq, _tile(seq, 256))

    qg = (jnp.tile(fx_qnorm, FX_HEADS) * (HEAD_DIM ** -0.5 * LOG2E)).reshape(1, FX_WIDTH)
    kg = jnp.tile(fx_knorm, FX_HEADS).reshape(1, FX_WIDTH)
    qa, ka = _fxprep(fx_qkv, small, _pad_lanes(fx_f_bias, LANE_F), qg, kg, bd, batch, seq, _tile(seq, 512))
    o_fx = _fxattn(qa, ka, fx_qkv, batch, seq, _tile(seq, 512), _tile(seq, 512), 2)

    pool_bd = jax.scipy.linalg.block_diag(*[pool_w[gi] for gi in range(POOL_GROUPS)]).astype(BF16)
    return _outproj(xf, o_dn, o_fx, pool_x, w_out.astype(BF16), pool_bd,
                    pool_scale.reshape(1, POOL_WIDTH), batch, seq, _tile(seq, 512))


def kernel(x, norm1, w_in, dn_conv, dn_a_log, dn_dt_bias, dn_onorm, fx_qnorm, fx_knorm, fx_f_bias, pool_w,
           pool_scale, w_out, norm2, ffn_gate, ffn_up, ffn_down, router, moe_gate, moe_up, moe_down):
    batch, seq, _ = x.shape
    n = batch * seq
    xf = x.reshape(n, D_MODEL)
    depth = norm1.shape[0]
    for layer in range(depth):
        xf = _mixer(xf, batch, seq, norm1[layer], w_in[layer], dn_conv[layer], dn_a_log[layer],
                    dn_dt_bias[layer], dn_onorm[layer], fx_qnorm[layer], fx_knorm[layer],
                    fx_f_bias[layer], pool_w[layer], pool_scale[layer], w_out[layer])
        g2 = norm2[layer].reshape(1, D_MODEL)
        j = layer // 2
        if layer % 2 == 0:
            xf = _ffn(xf, g2, ffn_gate[j].astype(BF16), ffn_up[j].astype(BF16), ffn_down[j].astype(BF16),
                      _tile(n, 512), D_FF // 2)
        else:
            r = jnp.concatenate([router[j], jnp.zeros((D_MODEL, LANES - N_EXPERTS), F32)], axis=1)
            r_hi = r.astype(BF16)
            r_lo = (r - r_hi.astype(F32)).astype(BF16)
            xf = _moe(xf, g2, r_hi, r_lo, moe_gate[j].astype(BF16), moe_up[j].astype(BF16),
                      moe_down[j].astype(BF16), _tile(n, 1024))
    return xf.reshape(batch, seq, D_MODEL)
```

```python
import functools

import jax
import jax.numpy as jnp
import numpy as np
from jax import lax
from jax.experimental import pallas as pl
from jax.experimental.pallas import tpu as pltpu

F32 = jnp.float32
BF16 = jnp.bfloat16

D_MODEL = 1024
HEAD_DIM = 64
DN_HEADS = 6
DN_WIDTH = DN_HEADS * HEAD_DIM
FX_HEADS = 6
FX_WIDTH = FX_HEADS * HEAD_DIM
POOL_GROUPS = 4
POOL_GROUP_DIM = 64
POOL_WIDTH = POOL_GROUPS * POOL_GROUP_DIM
POOL_WINDOWS = (2, 4, 8, 16)
CONV_WIDTH = 4
DN_CHUNK = 64
D_FF = 2816
N_EXPERTS = 8
D_FF_EXPERT = 1536
EPS = 1e-6
LANES = 128
NEG_BIG = -1e30
LOG2E = 1.4426950408889634

QKV_W = 3 * DN_WIDTH
COL_DN_QKV = 0
COL_DN_Z = COL_DN_QKV + QKV_W
COL_FX_QKV = COL_DN_Z + DN_WIDTH
COL_POOL = COL_FX_QKV + QKV_W
N_BIG = COL_POOL + POOL_WIDTH
LANE_A = 0
LANE_B = DN_HEADS
LANE_F = 2 * DN_HEADS

VMEM_LIMIT = 56 * 1024 * 1024


def _mm(a, b):
    return jnp.dot(a.astype(BF16), b.astype(BF16), preferred_element_type=F32)


def _mm_nt(a, b):
    return lax.dot_general(a.astype(BF16), b.astype(BF16), (((1,), (1,)), ((), ())),
                           preferred_element_type=F32)


def _mm_tn(a, b):
    return lax.dot_general(a.astype(BF16), b.astype(BF16), (((0,), (0,)), ((), ())),
                           preferred_element_type=F32)


def _split_mm(a, b_bf16):
    hi = a.astype(BF16)
    lo = (a - hi.astype(F32)).astype(BF16)
    return (jnp.dot(hi, b_bf16, preferred_element_type=F32)
            + jnp.dot(lo, b_bf16, preferred_element_type=F32))


def _sigmoid(x):
    return 1.0 / (1.0 + jnp.exp(-x))


def _softplus(x):
    return jnp.maximum(x, 0.0) + jnp.log(1.0 + jnp.exp(-jnp.abs(x)))


def _cumsum_rows(x, period):
    ridx = lax.broadcasted_iota(jnp.int32, x.shape, 0) & (period - 1)
    s = 1
    while s < period:
        x = x + jnp.where(ridx >= s, pltpu.roll(x, s, axis=0), 0.0)
        s *= 2
    return x


SRC_AB = 4 * DN_WIDTH
SRC_FX_QKV = SRC_AB + 2 * DN_HEADS
SRC_F = SRC_FX_QKV + QKV_W
SRC_POOL = SRC_F + FX_HEADS
N_IN = SRC_POOL + POOL_WIDTH


def _winprep_kernel(w_ref, big_ref, small_ref):
    big_ref[:, COL_DN_QKV:COL_FX_QKV] = w_ref[:, 0:SRC_AB].astype(BF16)
    big_ref[:, COL_FX_QKV:COL_POOL] = w_ref[:, SRC_FX_QKV:SRC_F].astype(BF16)
    big_ref[:, COL_POOL:N_BIG] = w_ref[:, SRC_POOL:N_IN].astype(BF16)
    lane = lax.broadcasted_iota(jnp.int32, small_ref.shape, 1)
    ab = w_ref[:, SRC_AB:SRC_AB + LANES]
    ff = w_ref[:, SRC_F - LANE_F:SRC_F - LANE_F + LANES]
    small_ref[...] = jnp.where(lane < LANE_F, ab, jnp.where(lane < LANE_F + FX_HEADS, ff, 0.0)).astype(BF16)


def _winprep(w_in):
    assert w_in.shape == (D_MODEL, N_IN) and (SRC_F - LANE_F) % LANES == 0
    rows = 256
    return pl.pallas_call(
        _winprep_kernel,
        grid=(D_MODEL // rows,),
        in_specs=[pl.BlockSpec((rows, N_IN), lambda i: (i, 0))],
        out_specs=(pl.BlockSpec((rows, N_BIG), lambda i: (i, 0)),
                   pl.BlockSpec((rows, LANES), lambda i: (i, 0))),
        out_shape=(jax.ShapeDtypeStruct((D_MODEL, N_BIG), BF16),
                   jax.ShapeDtypeStruct((D_MODEL, LANES), BF16)),
        compiler_params=pltpu.CompilerParams(dimension_semantics=("arbitrary",),
                                             vmem_limit_bytes=VMEM_LIMIT),
        name="winprep",
    )(w_in)


def _inproj_kernel(x_ref, g_ref, w_ref, ws_ref, dnqkv_ref, dnz_ref, fxqkv_ref, pool_ref, small_ref):
    x = x_ref[...]
    ms = jnp.mean(x * x, axis=-1, keepdims=True)
    h = (x * lax.rsqrt(ms + EPS) * g_ref[...]).astype(BF16)

    def proj(lo, hi):
        return jnp.dot(h, w_ref[:, lo:hi], preferred_element_type=F32)

    dnqkv_ref[...] = proj(COL_DN_QKV, COL_DN_Z).astype(BF16)
    dnz_ref[...] = proj(COL_DN_Z, COL_FX_QKV).astype(BF16)
    fxqkv_ref[...] = proj(COL_FX_QKV, COL_POOL).astype(BF16)
    pool_ref[...] = proj(COL_POOL, N_BIG).astype(BF16)
    small_ref[...] = jnp.dot(h, ws_ref[...], preferred_element_type=F32)


def _inproj(xf, g, w_big, w_small, tm):
    n = xf.shape[0]
    out_shape = (
        jax.ShapeDtypeStruct((n, QKV_W), BF16),
        jax.ShapeDtypeStruct((n, DN_WIDTH), BF16),
        jax.ShapeDtypeStruct((n, QKV_W), BF16),
        jax.ShapeDtypeStruct((n, POOL_WIDTH), BF16),
        jax.ShapeDtypeStruct((n, LANES), F32),
    )
    row = lambda w: pl.BlockSpec((tm, w), lambda i: (i, 0))
    full = lambda a: pl.BlockSpec(a.shape, lambda i: (0,) * a.ndim)
    return pl.pallas_call(
        _inproj_kernel,
        grid=(n // tm,),
        in_specs=[row(D_MODEL), full(g), full(w_big), full(w_small)],
        out_specs=(row(QKV_W), row(DN_WIDTH), row(QKV_W), row(POOL_WIDTH), row(LANES)),
        out_shape=out_shape,
        compiler_params=pltpu.CompilerParams(dimension_semantics=("arbitrary",),
                                             vmem_limit_bytes=VMEM_LIMIT),
        name="inproj",
    )(xf, g, w_big, w_small)


def _dotf(a, b):
    return jnp.dot(a, b, preferred_element_type=F32)


def _unit_lower_inverses(a_list, masks_ref, eye):
    m8 = masks_ref[0]
    d = [a * m8 for a in a_list]
    db = [x.astype(BF16) for x in d]
    d2 = [_dotf(x, x).astype(BF16) for x in db]
    d4 = [_dotf(x, x).astype(BF16) for x in d2]
    x = [eye - dd for dd in d]
    x = [xx + _dotf(xx.astype(BF16), y) for xx, y in zip(x, d2)]
    x = [xx + _dotf(xx.astype(BF16), y) for xx, y in zip(x, d4)]
    for level in (1, 2, 3):
        me = masks_ref[level]
        e = [(a * me).astype(BF16) for a in a_list]
        xb = [xx.astype(BF16) for xx in x]
        ex = [_dotf(ee, xx).astype(BF16) for ee, xx in zip(e, xb)]
        x = [xx - _dotf(xxb, eex) for xx, xxb, eex in zip(x, xb, ex)]
    return x


def _dn_kernel(qkv_ref, z_ref, small_ref, convw_ref, gpar_ref, onorm_ref, bd_ref, ea_ref, eb_ref, masks_ref,
               o_ref, ext_ref, s_ref, oscr_ref, *, tc):
    i = pl.program_id(1)
    n_chunks = tc // DN_CHUNK
    heads = range(DN_HEADS)

    @pl.when(i == 0)
    def _():
        ext_ref[0:8, :] = jnp.zeros((8, QKV_W), F32)
        s_ref[...] = jnp.zeros_like(s_ref)

    x = qkv_ref[...].astype(F32)
    ext_ref[8:8 + tc, :] = x
    w = convw_ref[...]
    y = x * w[CONV_WIDTH - 1:CONV_WIDTH, :]
    for j in range(CONV_WIDTH - 1):
        y = y + ext_ref[pl.ds(8 - (CONV_WIDTH - 1) + j, tc), :] * w[j:j + 1, :]
    ext_ref[0:8, :] = x[tc - 8:tc, :]
    y = y * _sigmoid(y)

    bd = bd_ref[...]
    q = y[:, 0:DN_WIDTH]
    k = y[:, DN_WIDTH:2 * DN_WIDTH]
    v = y[:, 2 * DN_WIDTH:3 * DN_WIDTH]
    q = q * lax.rsqrt(_split_mm(q * q, bd) + EPS) * (HEAD_DIM ** -0.5)
    k = k * lax.rsqrt(_split_mm(k * k, bd) + EPS)

    sm = small_ref[...]
    gp = gpar_ref[...]
    g = -jnp.exp(gp[0:1, :]) * _softplus(sm + gp[1:2, :])
    beta = _sigmoid(sm)
    gcum = _cumsum_rows(g, DN_CHUNK)
    chunk_of_row = lax.broadcasted_iota(jnp.int32, (tc, LANES), 0) // DN_CHUNK
    glast = jnp.broadcast_to(gcum[tc - 1:tc, :], (tc, LANES))
    for c in range(n_chunks - 2, -1, -1):
        glast = jnp.where(chunk_of_row == c, gcum[(c + 1) * DN_CHUNK - 1:(c + 1) * DN_CHUNK, :], glast)

    ea = ea_ref[...]
    eb = eb_ref[...]
    beta_w = _split_mm(beta, eb)
    eg_w = _split_mm(jnp.exp(gcum), ea)
    ekd_w = _split_mm(jnp.exp(glast - gcum), ea)
    egl_w = _split_mm(jnp.exp(glast), ea)
    kb = k * beta_w
    vb = v * beta_w
    kbg = kb * eg_w
    q_dec = q * eg_w
    k_dec = k * ekd_w
    gcum_t = gcum.T

    def hs(a, h):
        return a[:, h * HEAD_DIM:(h + 1) * HEAD_DIM]

    r_i = lax.broadcasted_iota(jnp.int32, (tc, tc), 0)
    c_i = lax.broadcasted_iota(jnp.int32, (tc, tc), 1)
    same_chunk = (r_i // DN_CHUNK) == (c_i // DN_CHUNK)
    causal = jnp.logical_and(same_chunk, r_i >= c_i)
    eye = jnp.where(r_i == c_i, 1.0, 0.0)

    decay = []
    for h in heads:
        diff = gcum[:, LANE_A + h:LANE_A + h + 1] - gcum_t[LANE_A + h:LANE_A + h + 1, :]
        decay.append(jnp.where(causal, jnp.exp(jnp.where(causal, diff, 0.0)), 0.0))
    prod = [_mm_nt(jnp.concatenate([hs(q, h), hs(kb, h)], axis=0), hs(k, h)) for h in heads]
    qk = [(prod[h][0:tc] * decay[h]).astype(BF16) for h in heads]
    a_mat = [prod[h][tc:2 * tc] * decay[h] * (1.0 - eye) for h in heads]
    t_inv = _unit_lower_inverses(a_mat, masks_ref, eye)
    uw = [_dotf(t_inv[h].astype(BF16),
                jnp.concatenate([hs(vb, h), hs(kbg, h)], axis=1).astype(BF16)).astype(BF16) for h in heads]
    k_bd = [jnp.where(same_chunk, jnp.concatenate([hs(k_dec, h)] * n_chunks, axis=1), 0.0).astype(BF16)
            for h in heads]
    ktuw = [lax.dot_general(k_bd[h], uw[h], (((0,), (0,)), ((), ())), preferred_element_type=F32)
            for h in heads]
    qkuw = [_dotf(qk[h], uw[h]) for h in heads]

    for c in range(n_chunks):
        r0 = c * DN_CHUNK
        r1 = r0 + DN_CHUNK
        s_all = s_ref[...]
        lhs = [jnp.concatenate([hs(q_dec, h)[r0:r1] - qkuw[h][r0:r1, HEAD_DIM:2 * HEAD_DIM],
                                ktuw[h][r0:r1, HEAD_DIM:2 * HEAD_DIM]], axis=0) for h in heads]
        res = [_mm(lhs[h], hs(s_all, h)) for h in heads]
        oscr_ref[r0:r1, :] = jnp.concatenate(
            [res[h][0:DN_CHUNK] + qkuw[h][r0:r1, 0:HEAD_DIM] for h in heads], axis=1)
        s_ref[...] = s_all * egl_w[r0:r0 + 1, :] + jnp.concatenate(
            [ktuw[h][r0:r1, 0:HEAD_DIM] - res[h][DN_CHUNK:2 * DN_CHUNK] for h in heads], axis=1)

    o = oscr_ref[...]
    ms = _split_mm(o * o, bd) * (1.0 / HEAD_DIM)
    o = o * lax.rsqrt(ms + EPS) * onorm_ref[...]
    zz = z_ref[...].astype(F32)
    o_ref[...] = (o * (zz * _sigmoid(zz))).astype(BF16)


def _dn_constants(tc):
    ea = np.zeros((LANES, DN_WIDTH), np.float32)
    eb = np.zeros((LANES, DN_WIDTH), np.float32)
    for h in range(DN_HEADS):
        ea[LANE_A + h, h * HEAD_DIM:(h + 1) * HEAD_DIM] = 1.0
        eb[LANE_B + h, h * HEAD_DIM:(h + 1) * HEAD_DIM] = 1.0
    r = np.arange(tc)[:, None]
    c = np.arange(tc)[None, :]
    masks = [(r >> 3) == (c >> 3)]
    for s in (3, 4, 5):
        masks.append(((r >> s) ^ (c >> s)) == 1)
    masks = np.stack(masks).astype(np.float32)
    return jnp.asarray(ea, BF16), jnp.asarray(eb, BF16), jnp.asarray(masks, F32)


def _deltanet(dn_qkv, dn_z, small, conv_w, gpar, onorm, bd, batch, seq, tc):
    n = batch * seq
    nt = seq // tc
    ea, eb, masks = _dn_constants(tc)
    row = lambda w: pl.BlockSpec((tc, w), lambda b, i: (b * nt + i, 0))
    full = lambda a: pl.BlockSpec(a.shape, lambda b, i: (0,) * a.ndim)
    return pl.pallas_call(
        functools.partial(_dn_kernel, tc=tc),
        grid=(batch, nt),
        in_specs=[row(QKV_W), row(DN_WIDTH), row(LANES), full(conv_w), full(gpar), full(onorm), full(bd),
                  full(ea), full(eb), full(masks)],
        out_specs=row(DN_WIDTH),
        out_shape=jax.ShapeDtypeStruct((n, DN_WIDTH), BF16),
        scratch_shapes=[
            pltpu.VMEM((tc + 8, QKV_W), F32),
            pltpu.VMEM((HEAD_DIM, DN_WIDTH), F32),
            pltpu.VMEM((tc, DN_WIDTH), F32),
        ],
        compiler_params=pltpu.CompilerParams(dimension_semantics=("arbitrary", "arbitrary"),
                                             vmem_limit_bytes=VMEM_LIMIT),
        name="deltanet",
    )(dn_qkv, dn_z, small, conv_w, gpar, onorm, bd, ea, eb, masks)


def _fxprep_kernel(qkv_ref, small_ref, fbias_ref, qg_ref, kg_ref, bd_ref, qa_ref, ka_ref, carry_ref, *, tc):
    i = pl.program_id(1)

    @pl.when(i == 0)
    def _():
        carry_ref[...] = jnp.zeros_like(carry_ref)

    bd = bd_ref[...]
    q = qkv_ref[:, 0:FX_WIDTH].astype(F32)
    k = qkv_ref[:, FX_WIDTH:2 * FX_WIDTH].astype(F32)
    q = q * lax.rsqrt(_split_mm(q * q, bd) * (1.0 / HEAD_DIM) + EPS) * qg_ref[...]
    k = k * lax.rsqrt(_split_mm(k * k, bd) * (1.0 / HEAD_DIM) + EPS) * kg_ref[...]

    logf = -_softplus(-(small_ref[...] + fbias_ref[...]))
    c = _cumsum_rows(logf, tc) + carry_ref[...]
    carry_ref[...] = c[tc - 1:tc, :]
    c = c * LOG2E

    c_hi = c.astype(BF16).astype(F32)
    r1 = c - c_hi
    c_mid = r1.astype(BF16).astype(F32)
    c_lo = (r1 - c_mid).astype(BF16).astype(F32)

    li = lax.broadcasted_iota(jnp.int32, (tc, HEAD_DIM), 1)
    for h in range(FX_HEADS):
        lane = LANE_F + h
        shape = (tc, HEAD_DIM)
        hi = jnp.broadcast_to(c_hi[:, lane:lane + 1], shape)
        mid = jnp.broadcast_to(c_mid[:, lane:lane + 1], shape)
        lo = jnp.broadcast_to(c_lo[:, lane:lane + 1], shape)
        q_ext = jnp.where(li == 0, hi, jnp.where(li == 1, mid, jnp.where(li == 2, lo,
                          jnp.where(li < 6, 1.0, 0.0))))
        k_ext = jnp.where(li < 3, 1.0, jnp.where(li == 3, -hi, jnp.where(li == 4, -mid,
                          jnp.where(li == 5, -lo, 0.0))))
        c0 = h * HEAD_DIM
        qa_ref[:, h * LANES:(h + 1) * LANES] = jnp.concatenate(
            [q[:, c0:c0 + HEAD_DIM], q_ext], axis=1).astype(BF16)
        ka_ref[:, h * LANES:(h + 1) * LANES] = jnp.concatenate(
            [k[:, c0:c0 + HEAD_DIM], k_ext], axis=1).astype(BF16)


def _fxprep(fx_qkv, small, fbias, qg, kg, bd, batch, seq, tc):
    n = batch * seq
    nt = seq // tc
    row = lambda w: pl.BlockSpec((tc, w), lambda b, i: (b * nt + i, 0))
    full = lambda a: pl.BlockSpec(a.shape, lambda b, i: (0,) * a.ndim)
    out = jax.ShapeDtypeStruct((n, FX_HEADS * LANES), BF16)
    return pl.pallas_call(
        functools.partial(_fxprep_kernel, tc=tc),
        grid=(batch, nt),
        in_specs=[row(QKV_W), row(LANES), full(fbias), full(qg), full(kg), full(bd)],
        out_specs=(row(FX_HEADS * LANES), row(FX_HEADS * LANES)),
        out_shape=(out, out),
        scratch_shapes=[pltpu.VMEM((1, LANES), F32)],
        compiler_params=pltpu.CompilerParams(dimension_semantics=("arbitrary", "arbitrary"),
                                             vmem_limit_bytes=VMEM_LIMIT),
        name="fxprep",
    )(fx_qkv, small, fbias, qg, kg, bd)


def _fxattn_kernel(qa_ref, ka_ref, v_ref, o_ref, *, seq, tq, tk, nsplit):
    w = tq // nsplit
    chains = [(hh, qs) for hh in range(2) for qs in range(nsplit)]
    blocks = [(qi, j) for qi in range(seq // tq) for j in range((qi + 1) * (tq // tk))]
    last_of_tile = {qi: (qi + 1) * (tq // tk) - 1 for qi in range(seq // tq)}
    state = {}
    scores = {}
    probs = {}

    def stage_scores(g):
        qi, j = blocks[g]
        q0, k0 = qi * tq, j * tk
        out = []
        for hh, qs in chains:
            qa = qa_ref[q0 + qs * w:q0 + (qs + 1) * w, hh * LANES:(hh + 1) * LANES]
            ka = ka_ref[k0:k0 + tk, hh * LANES:(hh + 1) * LANES]
            sc = lax.dot_general(ka, qa, (((1,), (1,)), ((), ())), preferred_element_type=F32)
            first_q = q0 + qs * w
            if k0 + tk - 1 > first_q:
                row = lax.broadcasted_iota(jnp.int32, sc.shape, 0)
                col = lax.broadcasted_iota(jnp.int32, sc.shape, 1)
                sc = jnp.where(row + (k0 - first_q) <= col, sc, NEG_BIG)
            out.append(sc)
        scores[g] = out

    def stage_softmax(g):
        qi, j = blocks[g]
        if j == 0:
            state[qi] = [(jnp.full((1, w), NEG_BIG, F32), jnp.zeros((1, w), F32),
                          jnp.zeros((HEAD_DIM, w), F32)) for _ in chains]
        out = []
        for ci in range(len(chains)):
            m_prev, l_prev, acc = state[qi][ci]
            sc = scores[g][ci]
            m_new = jnp.maximum(m_prev, jnp.max(sc, axis=0, keepdims=True))
            a = jnp.exp2(m_prev - m_new)
            p = jnp.exp2(sc - m_new)
            state[qi][ci] = (m_new, a * l_prev + jnp.sum(p, axis=0, keepdims=True), acc)
            out.append((p.astype(BF16), a))
        del scores[g]
        probs[g] = out

    def stage_values(g):
        qi, j = blocks[g]
        k0 = j * tk
        for ci, (hh, qs) in enumerate(chains):
            p, a = probs[g][ci]
            m_cur, l_cur, acc = state[qi][ci]
            vh = v_ref[k0:k0 + tk, hh * HEAD_DIM:(hh + 1) * HEAD_DIM]
            pv = lax.dot_general(vh, p, (((0,), (0,)), ((), ())), preferred_element_type=F32)
            state[qi][ci] = (m_cur, l_cur, a * acc + pv)
        del probs[g]
        if j == last_of_tile[qi]:
            cols = []
            for hh in range(2):
                per_q = [state[qi][hh * nsplit + qs] for qs in range(nsplit)]
                o_t = jnp.concatenate([acc / l_cur for (_, l_cur, acc) in per_q], axis=1)
                cols.append(o_t.T)
            o_ref[qi * tq:(qi + 1) * tq, :] = jnp.concatenate(cols, axis=1).astype(BF16)
            del state[qi]

    n_blocks = len(blocks)
    for t in range(n_blocks + 2):
        if t < n_blocks:
            stage_scores(t)
        if 0 <= t - 1 < n_blocks:
            stage_softmax(t - 1)
        if 0 <= t - 2 < n_blocks:
            stage_values(t - 2)


def _fxattn(qa, ka, fx_qkv, batch, seq, tq, tk, nsplit):
    n = batch * seq
    v_col0 = (2 * FX_WIDTH) // LANES
    return pl.pallas_call(
        functools.partial(_fxattn_kernel, seq=seq, tq=tq, tk=tk, nsplit=nsplit),
        grid=(batch, FX_HEADS // 2),
        in_specs=[
            pl.BlockSpec((seq, 2 * LANES), lambda b, hp: (b, hp)),
            pl.BlockSpec((seq, 2 * LANES), lambda b, hp: (b, hp)),
            pl.BlockSpec((seq, LANES), lambda b, hp: (b, v_col0 + hp)),
        ],
        out_specs=pl.BlockSpec((seq, LANES), lambda b, hp: (b, hp)),
        out_shape=jax.ShapeDtypeStruct((n, FX_WIDTH), BF16),
        compiler_params=pltpu.CompilerParams(dimension_semantics=("arbitrary", "arbitrary"),
                                             vmem_limit_bytes=VMEM_LIMIT),
        name="fxattn",
    )(qa, ka, fx_qkv)


def _outproj_kernel(x_ref, odn_ref, ofx_ref, px_ref, wout_ref, pbd_ref, pscale_ref, xo_ref, ext_ref, *, tc):
    i = pl.program_id(1)
    wmax = POOL_WINDOWS[-1]

    @pl.when(i == 0)
    def _():
        ext_ref[0:wmax, :] = jnp.zeros((wmax, POOL_WIDTH), F32)

    xp = px_ref[...].astype(F32)
    ext_ref[wmax:wmax + tc, :] = xp
    acc = xp
    sums = {}
    for j in range(1, wmax):
        acc = acc + ext_ref[pl.ds(wmax - j, tc), :]
        if j + 1 in POOL_WINDOWS:
            sums[j + 1] = acc
    ext_ref[0:wmax, :] = xp[tc - wmax:tc, :]

    grp = lax.broadcasted_iota(jnp.int32, (tc, POOL_WIDTH), 1) // POOL_GROUP_DIM
    t1 = lax.broadcasted_iota(jnp.int32, (tc, POOL_WIDTH), 0) + (i * tc + 1)
    wsum = sums[POOL_WINDOWS[-1]]
    wlen = jnp.full((tc, POOL_WIDTH), POOL_WINDOWS[-1], jnp.int32)
    for gi in range(POOL_GROUPS - 2, -1, -1):
        wsum = jnp.where(grp == gi, sums[POOL_WINDOWS[gi]], wsum)
        wlen = jnp.where(grp == gi, POOL_WINDOWS[gi], wlen)
    count = jnp.minimum(t1, wlen).astype(F32)
    y = wsum / count - xp
    pooled = _mm(y, pbd_ref[...]) * pscale_ref[...]

    out = x_ref[...]
    out = out + jnp.dot(odn_ref[...], wout_ref[0:DN_WIDTH, :], preferred_element_type=F32)
    out = out + jnp.dot(ofx_ref[...], wout_ref[DN_WIDTH:DN_WIDTH + FX_WIDTH, :], preferred_element_type=F32)
    out = out + jnp.dot(pooled.astype(BF16), wout_ref[DN_WIDTH + FX_WIDTH:D_MODEL, :],
                        preferred_element_type=F32)
    xo_ref[...] = out


def _outproj(xf, o_dn, o_fx, pool_x, w_out, pool_bd, pool_scale, batch, seq, tc):
    n = batch * seq
    nt = seq // tc
    row = lambda w: pl.BlockSpec((tc, w), lambda b, i: (b * nt + i, 0))
    full = lambda a: pl.BlockSpec(a.shape, lambda b, i: (0,) * a.ndim)
    return pl.pallas_call(
        functools.partial(_outproj_kernel, tc=tc),
        grid=(batch, nt),
        in_specs=[row(D_MODEL), row(DN_WIDTH), row(FX_WIDTH), row(POOL_WIDTH),
                  full(w_out), full(pool_bd), full(pool_scale)],
        out_specs=row(D_MODEL),
        out_shape=jax.ShapeDtypeStruct((n, D_MODEL), F32),
        scratch_shapes=[pltpu.VMEM((tc + POOL_WINDOWS[-1], POOL_WIDTH), F32)],
        compiler_params=pltpu.CompilerParams(dimension_semantics=("arbitrary", "arbitrary"),
                                             vmem_limit_bytes=VMEM_LIMIT),
        name="outproj",
    )(xf, o_dn, o_fx, pool_x, w_out, pool_bd, pool_scale)


def _ffn_kernel(x_ref, g_ref, wg_ref, wu_ref, wd_ref, o_ref, h_ref, acc_ref):
    f = pl.program_id(1)

    @pl.when(f == 0)
    def _():
        x = x_ref[...]
        ms = jnp.mean(x * x, axis=-1, keepdims=True)
        h_ref[...] = (x * lax.rsqrt(ms + EPS) * g_ref[...]).astype(BF16)
        acc_ref[...] = x

    h = h_ref[...]
    a = jnp.dot(h, wg_ref[...], preferred_element_type=F32)
    u = jnp.dot(h, wu_ref[...], preferred_element_type=F32)
    t = (a * _sigmoid(a) * u).astype(BF16)
    acc_ref[...] += jnp.dot(t, wd_ref[...], preferred_element_type=F32)

    @pl.when(f == pl.num_programs(1) - 1)
    def _():
        o_ref[...] = acc_ref[...]


def _ffn(xf, g, w_gate, w_up, w_down, tm, tf):
    n = xf.shape[0]
    nf = D_FF // tf
    return pl.pallas_call(
        _ffn_kernel,
        grid=(n // tm, nf),
        in_specs=[
            pl.BlockSpec((tm, D_MODEL), lambda i, f: (i, 0)),
            pl.BlockSpec((1, D_MODEL), lambda i, f: (0, 0)),
            pl.BlockSpec((D_MODEL, tf), lambda i, f: (0, f)),
            pl.BlockSpec((D_MODEL, tf), lambda i, f: (0, f)),
            pl.BlockSpec((tf, D_MODEL), lambda i, f: (f, 0)),
        ],
        out_specs=pl.BlockSpec((tm, D_MODEL), lambda i, f: (i, 0)),
        out_shape=jax.ShapeDtypeStruct((n, D_MODEL), F32),
        scratch_shapes=[pltpu.VMEM((tm, D_MODEL), BF16), pltpu.VMEM((tm, D_MODEL), F32)],
        compiler_params=pltpu.CompilerParams(dimension_semantics=("arbitrary", "arbitrary"),
                                             vmem_limit_bytes=VMEM_LIMIT),
        name="ffn",
    )(xf, g, w_gate, w_up, w_down)


MOE_SUB = 128
MOE_CHUNK = 256


def _moe_kernel(x_ref, g_ref, rhi_ref, rlo_ref, wg_ref, wu_ref, wd_ref, o_ref,
                h_ref, gate_ref, rank_ref, rankt_ref, cnt_ref, y_ref):
    e = pl.program_id(1)
    tb = x_ref.shape[0]
    lane = lax.broadcasted_iota(jnp.int32, (tb, LANES), 1)

    @pl.when(e == 0)
    def _():
        x = x_ref[...]
        ms = jnp.mean(x * x, axis=-1, keepdims=True)
        hf = x * lax.rsqrt(ms + EPS) * g_ref[...]
        h_hi = hf.astype(BF16)
        h_lo = (hf - h_hi.astype(F32)).astype(BF16)
        h_ref[...] = h_hi
        o_ref[...] = x
        logits = (jnp.dot(h_hi, rhi_ref[...], preferred_element_type=F32)
                  + jnp.dot(h_lo, rhi_ref[...], preferred_element_type=F32)
                  + jnp.dot(h_hi, rlo_ref[...], preferred_element_type=F32))
        logits = jnp.where(lane < N_EXPERTS, logits, NEG_BIG)
        ex = jnp.exp(logits - jnp.max(logits, axis=-1, keepdims=True))
        probs = ex / jnp.sum(ex, axis=-1, keepdims=True)
        p1 = jnp.max(probs, axis=-1, keepdims=True)
        i1 = jnp.min(jnp.where(probs == p1, lane, LANES), axis=-1, keepdims=True)
        rest = jnp.where(lane == i1, -1.0, probs)
        p2 = jnp.max(rest, axis=-1, keepdims=True)
        i2 = jnp.min(jnp.where(rest == p2, lane, LANES), axis=-1, keepdims=True)
        denom = p1 + p2
        gate_ref[...] = (jnp.where(lane == i1, p1 / denom, 0.0)
                         + jnp.where(lane == i2, p2 / denom, 0.0))
        sel = jnp.where(jnp.logical_or(lane == i1, lane == i2), 1.0, 0.0)
        cum = _cumsum_rows(sel, tb)
        rank = jnp.where(sel > 0.0, cum - sel, -1.0)
        rank_ref[...] = rank
        rankt_ref[...] = rank.T
        cnt_ref[...] = cum[tb - 1:tb, :]

    cnt = jnp.sum(jnp.where(lane[0:1, :] == e, cnt_ref[...], 0.0)).astype(jnp.int32)
    sub_i = lax.broadcasted_iota(jnp.int32, (LANES, tb), 0)
    rank_row = jnp.sum(jnp.where(sub_i == e, rankt_ref[...], 0.0), axis=0, keepdims=True)
    rank_col = jnp.sum(jnp.where(lane == e, rank_ref[...], 0.0), axis=-1, keepdims=True)
    gate_col = jnp.sum(jnp.where(lane == e, gate_ref[...], 0.0), axis=-1, keepdims=True)

    per_chunk = MOE_CHUNK // MOE_SUB
    for c in range(tb // MOE_CHUNK):
        for s in range(c * per_chunk, (c + 1) * per_chunk):
            r0 = s * MOE_SUB

            @pl.when(r0 < cnt)
            def _():
                rows = (lax.broadcasted_iota(jnp.int32, (MOE_SUB, tb), 0) + r0).astype(F32)
                onehot = jnp.where(rank_row == rows, 1.0, 0.0).astype(BF16)
                xe = jnp.dot(onehot, h_ref[...], preferred_element_type=F32).astype(BF16)
                a = jnp.dot(xe, wg_ref[...], preferred_element_type=F32)
                u = jnp.dot(xe, wu_ref[...], preferred_element_type=F32)
                t = (a * _sigmoid(a) * u).astype(BF16)
                y_ref[r0:r0 + MOE_SUB, :] = jnp.dot(t, wd_ref[...], preferred_element_type=F32).astype(BF16)

            if s > c * per_chunk:
                @pl.when(jnp.logical_and(c * MOE_CHUNK < cnt, r0 >= cnt))
                def _():
                    y_ref[r0:r0 + MOE_SUB, :] = jnp.zeros((MOE_SUB, D_MODEL), BF16)

        @pl.when(c * MOE_CHUNK < cnt)
        def _():
            cols = (lax.broadcasted_iota(jnp.int32, (tb, MOE_CHUNK), 1) + c * MOE_CHUNK).astype(F32)
            scatter = jnp.where(rank_col == cols, gate_col, 0.0).astype(BF16)
            o_ref[...] += jnp.dot(scatter, y_ref[c * MOE_CHUNK:(c + 1) * MOE_CHUNK, :],
                                  preferred_element_type=F32)


def _moe(xf, g, r_hi, r_lo, w_gate, w_up, w_down, tb):
    n = xf.shape[0]
    assert tb % MOE_CHUNK == 0 and MOE_CHUNK % MOE_SUB == 0
    return pl.pallas_call(
        _moe_kernel,
        grid=(n // tb, N_EXPERTS),
        in_specs=[
            pl.BlockSpec((tb, D_MODEL), lambda i, e: (i, 0)),
            pl.BlockSpec((1, D_MODEL), lambda i, e: (0, 0)),
            pl.BlockSpec((D_MODEL, LANES), lambda i, e: (0, 0)),
            pl.BlockSpec((D_MODEL, LANES), lambda i, e: (0, 0)),
            pl.BlockSpec((None, D_MODEL, D_FF_EXPERT), lambda i, e: (e, 0, 0)),
            pl.BlockSpec((None, D_MODEL, D_FF_EXPERT), lambda i, e: (e, 0, 0)),
            pl.BlockSpec((None, D_FF_EXPERT, D_MODEL), lambda i, e: (e, 0, 0)),
        ],
        out_specs=pl.BlockSpec((tb, D_MODEL), lambda i, e: (i, 0)),
        out_shape=jax.ShapeDtypeStruct((n, D_MODEL), F32),
        scratch_shapes=[
            pltpu.VMEM((tb, D_MODEL), BF16),
            pltpu.VMEM((tb, LANES), F32),
            pltpu.VMEM((tb, LANES), F32),
            pltpu.VMEM((LANES, tb), F32),
            pltpu.VMEM((1, LANES), F32),
            pltpu.VMEM((tb, D_MODEL), BF16),
        ],
        compiler_params=pltpu.CompilerParams(dimension_semantics=("arbitrary", "arbitrary"),
                                             vmem_limit_bytes=VMEM_LIMIT),
        name="moe",
    )(xf, g, r_hi, r_lo, w_gate, w_up, w_down)


def _pad_lanes(v, offset):
    return jnp.zeros((1, LANES), F32).at[0, offset:offset + v.shape[0]].set(v.astype(F32))


def _block_diag_ones(width, group):
    idx = np.arange(width) // group
    return jnp.asarray((idx[:, None] == idx[None, :]).astype(np.float32), BF16)


def _tile(n, pref):
    return pref if n % pref == 0 else n


def _mixer(xf, batch, seq, norm1, w_in, dn_conv, dn_a_log, dn_dt_bias, dn_onorm, fx_qnorm, fx_knorm,
           fx_f_bias, pool_w, pool_scale, w_out):
    n = batch * seq
    w_big, w_small = _winprep(w_in)

    dn_qkv, dn_z, fx_qkv, pool_x, small = _inproj(
        xf, norm1.reshape(1, D_MODEL), w_big, w_small, _tile(n, 512))

    bd = _block_diag_ones(DN_WIDTH, HEAD_DIM)
    gpar = jnp.concatenate([_pad_lanes(dn_a_log, LANE_A), _pad_lanes(dn_dt_bias, LANE_A)], axis=0)
    o_dn = _deltanet(dn_qkv, dn_z, small, dn_conv, gpar, jnp.tile(dn_onorm, DN_HEADS).reshape(1, DN_WIDTH),
                     bd, batch, seq, _tile(seq, 256))

    qg = (jnp.tile(fx_qnorm, FX_HEADS) * (HEAD_DIM ** -0.5 * LOG2E)).reshape(1, FX_WIDTH)
    kg = jnp.tile(fx_knorm, FX_HEADS).reshape(1, FX_WIDTH)
    qa, ka = _fxprep(fx_qkv, small, _pad_lanes(fx_f_bias, LANE_F), qg, kg, bd, batch, seq, _tile(seq, 512))
    o_fx = _fxattn(qa, ka, fx_qkv, batch, seq, _tile(seq, 512), _tile(seq, 512), 2)

    pool_bd = jax.scipy.linalg.block_diag(*[pool_w[gi] for gi in range(POOL_GROUPS)]).astype(BF16)
    return _outproj(xf, o_dn, o_fx, pool_x, w_out.astype(BF16), pool_bd,
                    pool_scale.reshape(1, POOL_WIDTH), batch, seq, _tile(seq, 512))


def kernel(x, norm1, w_in, dn_conv, dn_a_log, dn_dt_bias, dn_onorm, fx_qnorm, fx_knorm, fx_f_bias, pool_w,
           pool_scale, w_out, norm2, ffn_gate, ffn_up, ffn_down, router, moe_gate, moe_up, moe_down):
    batch, seq, _ = x.shape
    n = batch * seq
    xf = x.reshape(n, D_MODEL)
    depth = norm1.shape[0]
    for layer in range(depth):
        xf = _mixer(xf, batch, seq, norm1[layer], w_in[layer], dn_conv[layer], dn_a_log[layer],
                    dn_dt_bias[layer], dn_onorm[layer], fx_qnorm[layer], fx_knorm[layer],
                    fx_f_bias[layer], pool_w[layer], pool_scale[layer], w_out[layer])
        g2 = norm2[layer].reshape(1, D_MODEL)
        j = layer // 2
        if layer % 2 == 0:
            xf = _ffn(xf, g2, ffn_gate[j].astype(BF16), ffn_up[j].astype(BF16), ffn_down[j].astype(BF16),
                      _tile(n, 512), D_FF // 2)
        else:
            r = jnp.concatenate([router[j], jnp.zeros((D_MODEL, LANES - N_EXPERTS), F32)], axis=1)
            r_hi = r.astype(BF16)
            r_lo = (r - r_hi.astype(F32)).astype(BF16)
            xf = _moe(xf, g2, r_hi, r_lo, moe_gate[j].astype(BF16), moe_up[j].astype(BF16),
                      moe_down[j].astype(BF16), _tile(n, 1024))
    return xf.reshape(batch, seq, D_MODEL)
```

```python
import functools

import jax
import jax.numpy as jnp
import numpy as np
from jax import lax
from jax.experimental import pallas as pl
from jax.experimental.pallas import tpu as pltpu

F32 = jnp.float32
BF16 = jnp.bfloat16

D_MODEL = 1024
HEAD_DIM = 64
DN_HEADS = 6
DN_WIDTH = DN_HEADS * HEAD_DIM
FX_HEADS = 6
FX_WIDTH = FX_HEADS * HEAD_DIM
POOL_GROUPS = 4
POOL_GROUP_DIM = 64
POOL_WIDTH = POOL_GROUPS * POOL_GROUP_DIM
POOL_WINDOWS = (2, 4, 8, 16)
CONV_WIDTH = 4
DN_CHUNK = 64
D_FF = 2816
N_EXPERTS = 8
D_FF_EXPERT = 1536
EPS = 1e-6
LANES = 128
NEG_BIG = -1e30
LOG2E = 1.4426950408889634

QKV_W = 3 * DN_WIDTH
COL_DN_QKV = 0
COL_DN_Z = COL_DN_QKV + QKV_W
COL_FX_QKV = COL_DN_Z + DN_WIDTH
COL_POOL = COL_FX_QKV + QKV_W
N_BIG = COL_POOL + POOL_WIDTH
LANE_A = 0
LANE_B = DN_HEADS
LANE_F = 2 * DN_HEADS

VMEM_LIMIT = 56 * 1024 * 1024


def _mm(a, b):
    return jnp.dot(a.astype(BF16), b.astype(BF16), preferred_element_type=F32)


def _mm_nt(a, b):
    return lax.dot_general(a.astype(BF16), b.astype(BF16), (((1,), (1,)), ((), ())),
                           preferred_element_type=F32)


def _mm_tn(a, b):
    return lax.dot_general(a.astype(BF16), b.astype(BF16), (((0,), (0,)), ((), ())),
                           preferred_element_type=F32)


def _split_mm(a, b_bf16):
    hi = a.astype(BF16)
    lo = (a - hi.astype(F32)).astype(BF16)
    return (jnp.dot(hi, b_bf16, preferred_element_type=F32)
            + jnp.dot(lo, b_bf16, preferred_element_type=F32))


def _sigmoid(x):
    return 1.0 / (1.0 + jnp.exp(-x))


def _softplus(x):
    return jnp.maximum(x, 0.0) + jnp.log(1.0 + jnp.exp(-jnp.abs(x)))


def _cumsum_rows(x, period):
    ridx = lax.broadcasted_iota(jnp.int32, x.shape, 0) & (period - 1)
    s = 1
    while s < period:
        x = x + jnp.where(ridx >= s, pltpu.roll(x, s, axis=0), 0.0)
        s *= 2
    return x


SRC_AB = 4 * DN_WIDTH
SRC_FX_QKV = SRC_AB + 2 * DN_HEADS
SRC_F = SRC_FX_QKV + QKV_W
SRC_POOL = SRC_F + FX_HEADS
N_IN = SRC_POOL + POOL_WIDTH


def _winprep_kernel(w_ref, big_ref, small_ref):
    big_ref[:, COL_DN_QKV:COL_FX_QKV] = w_ref[:, 0:SRC_AB].astype(BF16)
    big_ref[:, COL_FX_QKV:COL_POOL] = w_ref[:, SRC_FX_QKV:SRC_F].astype(BF16)
    big_ref[:, COL_POOL:N_BIG] = w_ref[:, SRC_POOL:N_IN].astype(BF16)
    lane = lax.broadcasted_iota(jnp.int32, small_ref.shape, 1)
    ab = w_ref[:, SRC_AB:SRC_AB + LANES]
    ff = w_ref[:, SRC_F - LANE_F:SRC_F - LANE_F + LANES]
    small_ref[...] = jnp.where(lane < LANE_F, ab, jnp.where(lane < LANE_F + FX_HEADS, ff, 0.0)).astype(BF16)


def _winprep(w_in):
    assert w_in.shape == (D_MODEL, N_IN) and (SRC_F - LANE_F) % LANES == 0
    rows = 256
    return pl.pallas_call(
        _winprep_kernel,
        grid=(D_MODEL // rows,),
        in_specs=[pl.BlockSpec((rows, N_IN), lambda i: (i, 0))],
        out_specs=(pl.BlockSpec((rows, N_BIG), lambda i: (i, 0)),
                   pl.BlockSpec((rows, LANES), lambda i: (i, 0))),
        out_shape=(jax.ShapeDtypeStruct((D_MODEL, N_BIG), BF16),
                   jax.ShapeDtypeStruct((D_MODEL, LANES), BF16)),
        compiler_params=pltpu.CompilerParams(dimension_semantics=("arbitrary",),
                                             vmem_limit_bytes=VMEM_LIMIT),
        name="winprep",
    )(w_in)


def _inproj_kernel(x_ref, g_ref, w_ref, ws_ref, dnqkv_ref, dnz_ref, fxqkv_ref, pool_ref, small_ref):
    x = x_ref[...]
    ms = jnp.mean(x * x, axis=-1, keepdims=True)
    h = (x * lax.rsqrt(ms + EPS) * g_ref[...]).astype(BF16)

    def proj(lo, hi):
        return jnp.dot(h, w_ref[:, lo:hi], preferred_element_type=F32)

    dnqkv_ref[...] = proj(COL_DN_QKV, COL_DN_Z).astype(BF16)
    dnz_ref[...] = proj(COL_DN_Z, COL_FX_QKV).astype(BF16)
    fxqkv_ref[...] = proj(COL_FX_QKV, COL_POOL).astype(BF16)
    pool_ref[...] = proj(COL_POOL, N_BIG).astype(BF16)
    small_ref[...] = jnp.dot(h, ws_ref[...], preferred_element_type=F32)


def _inproj(xf, g, w_big, w_small, tm):
    n = xf.shape[0]
    out_shape = (
        jax.ShapeDtypeStruct((n, QKV_W), BF16),
        jax.ShapeDtypeStruct((n, DN_WIDTH), BF16),
        jax.ShapeDtypeStruct((n, QKV_W), BF16),
        jax.ShapeDtypeStruct((n, POOL_WIDTH), BF16),
        jax.ShapeDtypeStruct((n, LANES), F32),
    )
    row = lambda w: pl.BlockSpec((tm, w), lambda i: (i, 0))
    full = lambda a: pl.BlockSpec(a.shape, lambda i: (0,) * a.ndim)
    return pl.pallas_call(
        _inproj_kernel,
        grid=(n // tm,),
        in_specs=[row(D_MODEL), full(g), full(w_big), full(w_small)],
        out_specs=(row(QKV_W), row(DN_WIDTH), row(QKV_W), row(POOL_WIDTH), row(LANES)),
        out_shape=out_shape,
        compiler_params=pltpu.CompilerParams(dimension_semantics=("arbitrary",),
                                             vmem_limit_bytes=VMEM_LIMIT),
        name="inproj",
    )(xf, g, w_big, w_small)


def _dotf(a, b):
    return jnp.dot(a, b, preferred_element_type=F32)


def _unit_lower_inverses(a_list, masks_ref, eye):
    m8 = masks_ref[0]
    d = [a * m8 for a in a_list]
    db = [x.astype(BF16) for x in d]
    d2 = [_dotf(x, x).astype(BF16) for x in db]
    d4 = [_dotf(x, x).astype(BF16) for x in d2]
    x = [eye - dd for dd in d]
    x = [xx + _dotf(xx.astype(BF16), y) for xx, y in zip(x, d2)]
    x = [xx + _dotf(xx.astype(BF16), y) for xx, y in zip(x, d4)]
    for level in (1, 2, 3):
        me = masks_ref[level]
        e = [(a * me).astype(BF16) for a in a_list]
        xb = [xx.astype(BF16) for xx in x]
        ex = [_dotf(ee, xx).astype(BF16) for ee, xx in zip(e, xb)]
        x = [xx - _dotf(xxb, eex) for xx, xxb, eex in zip(x, xb, ex)]
    return x


def _dn_kernel(qkv_ref, z_ref, small_ref, convw_ref, gpar_ref, onorm_ref, bd_ref, ea_ref, eb_ref, masks_ref,
               o_ref, ext_ref, s_ref, oscr_ref, *, tc):
    i = pl.program_id(1)
    n_chunks = tc // DN_CHUNK
    heads = range(DN_HEADS)

    @pl.when(i == 0)
    def _():
        ext_ref[0:8, :] = jnp.zeros((8, QKV_W), F32)
        s_ref[...] = jnp.zeros_like(s_ref)

    x = qkv_ref[...].astype(F32)
    ext_ref[8:8 + tc, :] = x
    w = convw_ref[...]
    y = x * w[CONV_WIDTH - 1:CONV_WIDTH, :]
    for j in range(CONV_WIDTH - 1):
        y = y + ext_ref[pl.ds(8 - (CONV_WIDTH - 1) + j, tc), :] * w[j:j + 1, :]
    ext_ref[0:8, :] = x[tc - 8:tc, :]
    y = y * _sigmoid(y)

    bd = bd_ref[...]
    q = y[:, 0:DN_WIDTH]
    k = y[:, DN_WIDTH:2 * DN_WIDTH]
    v = y[:, 2 * DN_WIDTH:3 * DN_WIDTH]
    q = q * lax.rsqrt(_split_mm(q * q, bd) + EPS) * (HEAD_DIM ** -0.5)
    k = k * lax.rsqrt(_split_mm(k * k, bd) + EPS)

    sm = small_ref[...]
    gp = gpar_ref[...]
    g = -jnp.exp(gp[0:1, :]) * _softplus(sm + gp[1:2, :])
    beta = _sigmoid(sm)
    gcum = _cumsum_rows(g, DN_CHUNK)
    chunk_of_row = lax.broadcasted_iota(jnp.int32, (tc, LANES), 0) // DN_CHUNK
    glast = jnp.broadcast_to(gcum[tc - 1:tc, :], (tc, LANES))
    for c in range(n_chunks - 2, -1, -1):
        glast = jnp.where(chunk_of_row == c, gcum[(c + 1) * DN_CHUNK - 1:(c + 1) * DN_CHUNK, :], glast)

    ea = ea_ref[...]
    eb = eb_ref[...]
    beta_w = _split_mm(beta, eb)
    eg_w = _split_mm(jnp.exp(gcum), ea)
    ekd_w = _split_mm(jnp.exp(glast - gcum), ea)
    egl_w = _split_mm(jnp.exp(glast), ea)
    kb = k * beta_w
    vb = v * beta_w
    kbg = kb * eg_w
    q_dec = q * eg_w
    k_dec = k * ekd_w
    gcum_t = gcum.T

    def hs(a, h):
        return a[:, h * HEAD_DIM:(h + 1) * HEAD_DIM]

    r_i = lax.broadcasted_iota(jnp.int32, (tc, tc), 0)
    c_i = lax.broadcasted_iota(jnp.int32, (tc, tc), 1)
    same_chunk = (r_i // DN_CHUNK) == (c_i // DN_CHUNK)
    causal = jnp.logical_and(same_chunk, r_i >= c_i)
    eye = jnp.where(r_i == c_i, 1.0, 0.0)

    decay = []
    for h in heads:
        diff = gcum[:, LANE_A + h:LANE_A + h + 1] - gcum_t[LANE_A + h:LANE_A + h + 1, :]
        decay.append(jnp.where(causal, jnp.exp(jnp.where(causal, diff, 0.0)), 0.0))
    prod = [_mm_nt(jnp.concatenate([hs(q, h), hs(kb, h)], axis=0), hs(k, h)) for h in heads]
    qk = [(prod[h][0:tc] * decay[h]).astype(BF16) for h in heads]
    a_mat = [prod[h][tc:2 * tc] * decay[h] for h in heads]
    t_inv = _unit_lower_inverses(a_mat, masks_ref, eye)
    uw = [_dotf(t_inv[h].astype(BF16),
                jnp.concatenate([hs(vb, h), hs(kbg, h)], axis=1).astype(BF16)).astype(BF16) for h in heads]
    k_bd = [jnp.where(same_chunk, jnp.concatenate([hs(k_dec, h)] * n_chunks, axis=1), 0.0).astype(BF16)
            for h in heads]
    ktuw = [lax.dot_general(k_bd[h], uw[h], (((0,), (0,)), ((), ())), preferred_element_type=F32)
            for h in heads]
    qkuw = [_dotf(qk[h], uw[h]) for h in heads]

    for c in range(n_chunks):
        r0 = c * DN_CHUNK
        r1 = r0 + DN_CHUNK
        s_all = s_ref[...]
        lhs = [jnp.concatenate([hs(q_dec, h)[r0:r1] - qkuw[h][r0:r1, HEAD_DIM:2 * HEAD_DIM],
                                ktuw[h][r0:r1, HEAD_DIM:2 * HEAD_DIM]], axis=0) for h in heads]
        res = [_mm(lhs[h], hs(s_all, h)) for h in heads]
        oscr_ref[r0:r1, :] = jnp.concatenate(
            [res[h][0:DN_CHUNK] + qkuw[h][r0:r1, 0:HEAD_DIM] for h in heads], axis=1)
        s_ref[...] = s_all * egl_w[r0:r0 + 1, :] + jnp.concatenate(
            [ktuw[h][r0:r1, 0:HEAD_DIM] - res[h][DN_CHUNK:2 * DN_CHUNK] for h in heads], axis=1)

    o = oscr_ref[...]
    ms = _split_mm(o * o, bd) * (1.0 / HEAD_DIM)
    o = o * lax.rsqrt(ms + EPS) * onorm_ref[...]
    zz = z_ref[...].astype(F32)
    o_ref[...] = (o * (zz * _sigmoid(zz))).astype(BF16)


def _dn_constants(tc):
    ea = np.zeros((LANES, DN_WIDTH), np.float32)
    eb = np.zeros((LANES, DN_WIDTH), np.float32)
    for h in range(DN_HEADS):
        ea[LANE_A + h, h * HEAD_DIM:(h + 1) * HEAD_DIM] = 1.0
        eb[LANE_B + h, h * HEAD_DIM:(h + 1) * HEAD_DIM] = 1.0
    r = np.arange(tc)[:, None]
    c = np.arange(tc)[None, :]
    masks = [((r >> 3) == (c >> 3)) & (r != c)]
    for s in (3, 4, 5):
        masks.append(((r >> s) ^ (c >> s)) == 1)
    masks = np.stack(masks).astype(np.float32)
    return jnp.asarray(ea, BF16), jnp.asarray(eb, BF16), jnp.asarray(masks, F32)


def _deltanet(dn_qkv, dn_z, small, conv_w, gpar, onorm, bd, batch, seq, tc):
    n = batch * seq
    nt = seq // tc
    ea, eb, masks = _dn_constants(tc)
    row = lambda w: pl.BlockSpec((tc, w), lambda b, i: (b * nt + i, 0))
    full = lambda a: pl.BlockSpec(a.shape, lambda b, i: (0,) * a.ndim)
    return pl.pallas_call(
        functools.partial(_dn_kernel, tc=tc),
        grid=(batch, nt),
        in_specs=[row(QKV_W), row(DN_WIDTH), row(LANES), full(conv_w), full(gpar), full(onorm), full(bd),
                  full(ea), full(eb), full(masks)],
        out_specs=row(DN_WIDTH),
        out_shape=jax.ShapeDtypeStruct((n, DN_WIDTH), BF16),
        scratch_shapes=[
            pltpu.VMEM((tc + 8, QKV_W), F32),
            pltpu.VMEM((HEAD_DIM, DN_WIDTH), F32),
            pltpu.VMEM((tc, DN_WIDTH), F32),
        ],
        compiler_params=pltpu.CompilerParams(dimension_semantics=("arbitrary", "arbitrary"),
                                             vmem_limit_bytes=VMEM_LIMIT),
        name="deltanet",
    )(dn_qkv, dn_z, small, conv_w, gpar, onorm, bd, ea, eb, masks)


def _fxprep_kernel(qkv_ref, small_ref, fbias_ref, qg_ref, kg_ref, bd_ref, qa_ref, ka_ref, carry_ref, *, tc):
    i = pl.program_id(1)

    @pl.when(i == 0)
    def _():
        carry_ref[...] = jnp.zeros_like(carry_ref)

    bd = bd_ref[...]
    q = qkv_ref[:, 0:FX_WIDTH].astype(F32)
    k = qkv_ref[:, FX_WIDTH:2 * FX_WIDTH].astype(F32)
    q = q * lax.rsqrt(_split_mm(q * q, bd) * (1.0 / HEAD_DIM) + EPS) * qg_ref[...]
    k = k * lax.rsqrt(_split_mm(k * k, bd) * (1.0 / HEAD_DIM) + EPS) * kg_ref[...]

    logf = -_softplus(-(small_ref[...] + fbias_ref[...]))
    c = _cumsum_rows(logf, tc) + carry_ref[...]
    carry_ref[...] = c[tc - 1:tc, :]
    c = c * LOG2E

    c_hi = c.astype(BF16).astype(F32)
    r1 = c - c_hi
    c_mid = r1.astype(BF16).astype(F32)
    c_lo = (r1 - c_mid).astype(BF16).astype(F32)

    li = lax.broadcasted_iota(jnp.int32, (tc, HEAD_DIM), 1)
    for h in range(FX_HEADS):
        lane = LANE_F + h
        shape = (tc, HEAD_DIM)
        hi = jnp.broadcast_to(c_hi[:, lane:lane + 1], shape)
        mid = jnp.broadcast_to(c_mid[:, lane:lane + 1], shape)
        lo = jnp.broadcast_to(c_lo[:, lane:lane + 1], shape)
        q_ext = jnp.where(li == 0, hi, jnp.where(li == 1, mid, jnp.where(li == 2, lo,
                          jnp.where(li < 6, 1.0, 0.0))))
        k_ext = jnp.where(li < 3, 1.0, jnp.where(li == 3, -hi, jnp.where(li == 4, -mid,
                          jnp.where(li == 5, -lo, 0.0))))
        c0 = h * HEAD_DIM
        qa_ref[:, h * LANES:(h + 1) * LANES] = jnp.concatenate(
            [q[:, c0:c0 + HEAD_DIM], q_ext], axis=1).astype(BF16)
        ka_ref[:, h * LANES:(h + 1) * LANES] = jnp.concatenate(
            [k[:, c0:c0 + HEAD_DIM], k_ext], axis=1).astype(BF16)


def _fxprep(fx_qkv, small, fbias, qg, kg, bd, batch, seq, tc):
    n = batch * seq
    nt = seq // tc
    row = lambda w: pl.BlockSpec((tc, w), lambda b, i: (b * nt + i, 0))
    full = lambda a: pl.BlockSpec(a.shape, lambda b, i: (0,) * a.ndim)
    out = jax.ShapeDtypeStruct((n, FX_HEADS * LANES), BF16)
    return pl.pallas_call(
        functools.partial(_fxprep_kernel, tc=tc),
        grid=(batch, nt),
        in_specs=[row(QKV_W), row(LANES), full(fbias), full(qg), full(kg), full(bd)],
        out_specs=(row(FX_HEADS * LANES), row(FX_HEADS * LANES)),
        out_shape=(out, out),
        scratch_shapes=[pltpu.VMEM((1, LANES), F32)],
        compiler_params=pltpu.CompilerParams(dimension_semantics=("arbitrary", "arbitrary"),
                                             vmem_limit_bytes=VMEM_LIMIT),
        name="fxprep",
    )(fx_qkv, small, fbias, qg, kg, bd)


def _fxattn_kernel(qa_ref, ka_ref, v_ref, o_ref, *, seq, tq, tk, nsplit):
    w = tq // nsplit
    chains = [(hh, qs) for hh in range(2) for qs in range(nsplit)]
    blocks = [(qi, j) for qi in range(seq // tq) for j in range((qi + 1) * (tq // tk))]
    last_of_tile = {qi: (qi + 1) * (tq // tk) - 1 for qi in range(seq // tq)}
    state = {}
    scores = {}
    probs = {}

    def stage_scores(g):
        qi, j = blocks[g]
        q0, k0 = qi * tq, j * tk
        out = []
        for hh, qs in chains:
            first_q = q0 + qs * w
            n_keys = min(tk, first_q + w - k0)
            qa = qa_ref[first_q:first_q + w, hh * LANES:(hh + 1) * LANES]
            ka = ka_ref[k0:k0 + n_keys, hh * LANES:(hh + 1) * LANES]
            sc = lax.dot_general(ka, qa, (((1,), (1,)), ((), ())), preferred_element_type=F32)
            if k0 + n_keys - 1 > first_q:
                row = lax.broadcasted_iota(jnp.int32, sc.shape, 0)
                col = lax.broadcasted_iota(jnp.int32, sc.shape, 1)
                sc = jnp.where(row + (k0 - first_q) <= col, sc, NEG_BIG)
            out.append(sc)
        scores[g] = out

    def stage_softmax(g):
        qi, j = blocks[g]
        if j == 0:
            state[qi] = [(jnp.full((1, w), NEG_BIG, F32), jnp.zeros((1, w), F32),
                          jnp.zeros((HEAD_DIM, w), F32)) for _ in chains]
        out = []
        for ci in range(len(chains)):
            m_prev, l_prev, acc = state[qi][ci]
            sc = scores[g][ci]
            m_new = jnp.maximum(m_prev, jnp.max(sc, axis=0, keepdims=True))
            a = jnp.exp2(m_prev - m_new)
            p = jnp.exp2(sc - m_new)
            state[qi][ci] = (m_new, a * l_prev + jnp.sum(p, axis=0, keepdims=True), acc)
            out.append((p.astype(BF16), a))
        del scores[g]
        probs[g] = out

    def stage_values(g):
        qi, j = blocks[g]
        k0 = j * tk
        for ci, (hh, qs) in enumerate(chains):
            p, a = probs[g][ci]
            m_cur, l_cur, acc = state[qi][ci]
            vh = v_ref[k0:k0 + p.shape[0], hh * HEAD_DIM:(hh + 1) * HEAD_DIM]
            pv = lax.dot_general(vh, p, (((0,), (0,)), ((), ())), preferred_element_type=F32)
            state[qi][ci] = (m_cur, l_cur, a * acc + pv)
        del probs[g]
        if j == last_of_tile[qi]:
            cols = []
            for hh in range(2):
                per_q = [state[qi][hh * nsplit + qs] for qs in range(nsplit)]
                o_t = jnp.concatenate([acc / l_cur for (_, l_cur, acc) in per_q], axis=1)
                cols.append(o_t.T)
            o_ref[qi * tq:(qi + 1) * tq, :] = jnp.concatenate(cols, axis=1).astype(BF16)
            del state[qi]

    n_blocks = len(blocks)
    for t in range(n_blocks + 2):
        if t < n_blocks:
            stage_scores(t)
        if 0 <= t - 1 < n_blocks:
            stage_softmax(t - 1)
        if 0 <= t - 2 < n_blocks:
            stage_values(t - 2)


def _fxattn(qa, ka, fx_qkv, batch, seq, tq, tk, nsplit):
    n = batch * seq
    v_col0 = (2 * FX_WIDTH) // LANES
    return pl.pallas_call(
        functools.partial(_fxattn_kernel, seq=seq, tq=tq, tk=tk, nsplit=nsplit),
        grid=(batch, FX_HEADS // 2),
        in_specs=[
            pl.BlockSpec((seq, 2 * LANES), lambda b, hp: (b, hp)),
            pl.BlockSpec((seq, 2 * LANES), lambda b, hp: (b, hp)),
            pl.BlockSpec((seq, LANES), lambda b, hp: (b, v_col0 + hp)),
        ],
        out_specs=pl.BlockSpec((seq, LANES), lambda b, hp: (b, hp)),
        out_shape=jax.ShapeDtypeStruct((n, FX_WIDTH), BF16),
        compiler_params=pltpu.CompilerParams(dimension_semantics=("arbitrary", "arbitrary"),
                                             vmem_limit_bytes=VMEM_LIMIT),
        name="fxattn",
    )(qa, ka, fx_qkv)


def _outproj_kernel(x_ref, odn_ref, ofx_ref, px_ref, wout_ref, pbd_ref, pscale_ref, xo_ref, ext_ref, *, tc):
    i = pl.program_id(1)
    wmax = POOL_WINDOWS[-1]

    @pl.when(i == 0)
    def _():
        ext_ref[0:wmax, :] = jnp.zeros((wmax, POOL_WIDTH), F32)

    xp = px_ref[...].astype(F32)
    ext_ref[wmax:wmax + tc, :] = xp
    acc = xp
    sums = {}
    for j in range(1, wmax):
        acc = acc + ext_ref[pl.ds(wmax - j, tc), :]
        if j + 1 in POOL_WINDOWS:
            sums[j + 1] = acc
    ext_ref[0:wmax, :] = xp[tc - wmax:tc, :]

    grp = lax.broadcasted_iota(jnp.int32, (tc, POOL_WIDTH), 1) // POOL_GROUP_DIM
    t1 = lax.broadcasted_iota(jnp.int32, (tc, POOL_WIDTH), 0) + (i * tc + 1)
    wsum = sums[POOL_WINDOWS[-1]]
    wlen = jnp.full((tc, POOL_WIDTH), POOL_WINDOWS[-1], jnp.int32)
    for gi in range(POOL_GROUPS - 2, -1, -1):
        wsum = jnp.where(grp == gi, sums[POOL_WINDOWS[gi]], wsum)
        wlen = jnp.where(grp == gi, POOL_WINDOWS[gi], wlen)
    count = jnp.minimum(t1, wlen).astype(F32)
    y = wsum / count - xp
    pooled = _mm(y, pbd_ref[...]) * pscale_ref[...]

    out = x_ref[...]
    out = out + jnp.dot(odn_ref[...], wout_ref[0:DN_WIDTH, :], preferred_element_type=F32)
    out = out + jnp.dot(ofx_ref[...], wout_ref[DN_WIDTH:DN_WIDTH + FX_WIDTH, :], preferred_element_type=F32)
    out = out + jnp.dot(pooled.astype(BF16), wout_ref[DN_WIDTH + FX_WIDTH:D_MODEL, :],
                        preferred_element_type=F32)
    xo_ref[...] = out


def _outproj(xf, o_dn, o_fx, pool_x, w_out, pool_bd, pool_scale, batch, seq, tc):
    n = batch * seq
    nt = seq // tc
    row = lambda w: pl.BlockSpec((tc, w), lambda b, i: (b * nt + i, 0))
    full = lambda a: pl.BlockSpec(a.shape, lambda b, i: (0,) * a.ndim)
    return pl.pallas_call(
        functools.partial(_outproj_kernel, tc=tc),
        grid=(batch, nt),
        in_specs=[row(D_MODEL), row(DN_WIDTH), row(FX_WIDTH), row(POOL_WIDTH),
                  full(w_out), full(pool_bd), full(pool_scale)],
        out_specs=row(D_MODEL),
        out_shape=jax.ShapeDtypeStruct((n, D_MODEL), F32),
        scratch_shapes=[pltpu.VMEM((tc + POOL_WINDOWS[-1], POOL_WIDTH), F32)],
        compiler_params=pltpu.CompilerParams(dimension_semantics=("arbitrary", "arbitrary"),
                                             vmem_limit_bytes=VMEM_LIMIT),
        name="outproj",
    )(xf, o_dn, o_fx, pool_x, w_out, pool_bd, pool_scale)


def _ffn_kernel(x_ref, g_ref, wg_ref, wu_ref, wd_ref, o_ref, h_ref, acc_ref):
    f = pl.program_id(1)

    @pl.when(f == 0)
    def _():
        x = x_ref[...]
        ms = jnp.mean(x * x, axis=-1, keepdims=True)
        h_ref[...] = (x * lax.rsqrt(ms + EPS) * g_ref[...]).astype(BF16)
        acc_ref[...] = x

    h = h_ref[...]
    a = jnp.dot(h, wg_ref[...], preferred_element_type=F32)
    u = jnp.dot(h, wu_ref[...], preferred_element_type=F32)
    t = (a * _sigmoid(a) * u).astype(BF16)
    acc_ref[...] += jnp.dot(t, wd_ref[...], preferred_element_type=F32)

    @pl.when(f == pl.num_programs(1) - 1)
    def _():
        o_ref[...] = acc_ref[...]


def _ffn(xf, g, w_gate, w_up, w_down, tm, tf):
    n = xf.shape[0]
    nf = D_FF // tf
    return pl.pallas_call(
        _ffn_kernel,
        grid=(n // tm, nf),
        in_specs=[
            pl.BlockSpec((tm, D_MODEL), lambda i, f: (i, 0)),
            pl.BlockSpec((1, D_MODEL), lambda i, f: (0, 0)),
            pl.BlockSpec((D_MODEL, tf), lambda i, f: (0, f)),
            pl.BlockSpec((D_MODEL, tf), lambda i, f: (0, f)),
            pl.BlockSpec((tf, D_MODEL), lambda i, f: (f, 0)),
        ],
        out_specs=pl.BlockSpec((tm, D_MODEL), lambda i, f: (i, 0)),
        out_shape=jax.ShapeDtypeStruct((n, D_MODEL), F32),
        scratch_shapes=[pltpu.VMEM((tm, D_MODEL), BF16), pltpu.VMEM((tm, D_MODEL), F32)],
        compiler_params=pltpu.CompilerParams(dimension_semantics=("arbitrary", "arbitrary"),
                                             vmem_limit_bytes=VMEM_LIMIT),
        name="ffn",
    )(xf, g, w_gate, w_up, w_down)


MOE_SUB = 128
MOE_SUB_SMALL = 64
MOE_CHUNK = 256
TOP_K = 2


def _moe_subtiles(tb):
    balanced = TOP_K * tb // N_EXPERTS
    subs = []
    r = 0
    while r < tb:
        rows = MOE_SUB_SMALL if balanced <= r < balanced + MOE_SUB else MOE_SUB
        subs.append((r, rows))
        r += rows
    assert r == tb and all(a // MOE_CHUNK == (a + n - 1) // MOE_CHUNK for a, n in subs)
    return subs


def _moe_kernel(x_ref, g_ref, rhi_ref, rlo_ref, wg_ref, wu_ref, wd_ref, o_ref,
                h_ref, gate_ref, rank_ref, rankt_ref, cnt_ref, y_ref):
    e = pl.program_id(1)
    tb = x_ref.shape[0]
    lane = lax.broadcasted_iota(jnp.int32, (tb, LANES), 1)

    @pl.when(e == 0)
    def _():
        x = x_ref[...]
        ms = jnp.mean(x * x, axis=-1, keepdims=True)
        hf = x * lax.rsqrt(ms + EPS) * g_ref[...]
        h_hi = hf.astype(BF16)
        h_lo = (hf - h_hi.astype(F32)).astype(BF16)
        h_ref[...] = h_hi
        o_ref[...] = x
        logits = (jnp.dot(h_hi, rhi_ref[...], preferred_element_type=F32)
                  + jnp.dot(h_lo, rhi_ref[...], preferred_element_type=F32)
                  + jnp.dot(h_hi, rlo_ref[...], preferred_element_type=F32))
        logits = jnp.where(lane < N_EXPERTS, logits, NEG_BIG)
        ex = jnp.exp(logits - jnp.max(logits, axis=-1, keepdims=True))
        probs = ex / jnp.sum(ex, axis=-1, keepdims=True)
        p1 = jnp.max(probs, axis=-1, keepdims=True)
        i1 = jnp.min(jnp.where(probs == p1, lane, LANES), axis=-1, keepdims=True)
        rest = jnp.where(lane == i1, -1.0, probs)
        p2 = jnp.max(rest, axis=-1, keepdims=True)
        i2 = jnp.min(jnp.where(rest == p2, lane, LANES), axis=-1, keepdims=True)
        denom = p1 + p2
        gate_ref[...] = (jnp.where(lane == i1, p1 / denom, 0.0)
                         + jnp.where(lane == i2, p2 / denom, 0.0))
        sel = jnp.where(jnp.logical_or(lane == i1, lane == i2), 1.0, 0.0)
        cum = _cumsum_rows(sel, tb)
        rank = jnp.where(sel > 0.0, cum - sel, -1.0)
        rank_ref[...] = rank
        rankt_ref[...] = rank.T
        cnt_ref[...] = cum[tb - 1:tb, :]

    cnt = jnp.sum(jnp.where(lane[0:1, :] == e, cnt_ref[...], 0.0)).astype(jnp.int32)
    sub_i = lax.broadcasted_iota(jnp.int32, (LANES, tb), 0)
    rank_row = jnp.sum(jnp.where(sub_i == e, rankt_ref[...], 0.0), axis=0, keepdims=True)
    rank_col = jnp.sum(jnp.where(lane == e, rank_ref[...], 0.0), axis=-1, keepdims=True)
    gate_col = jnp.sum(jnp.where(lane == e, gate_ref[...], 0.0), axis=-1, keepdims=True)

    subs = _moe_subtiles(tb)
    for c in range(tb // MOE_CHUNK):
        for r0, n_rows in [sub for sub in subs if sub[0] // MOE_CHUNK == c]:

            @pl.when(r0 < cnt)
            def _(r0=r0, n_rows=n_rows):
                rows = (lax.broadcasted_iota(jnp.int32, (n_rows, tb), 0) + r0).astype(F32)
                onehot = jnp.where(rank_row == rows, 1.0, 0.0).astype(BF16)
                xe = jnp.dot(onehot, h_ref[...], preferred_element_type=F32).astype(BF16)
                a = jnp.dot(xe, wg_ref[...], preferred_element_type=F32)
                u = jnp.dot(xe, wu_ref[...], preferred_element_type=F32)
                t = (a * _sigmoid(a) * u).astype(BF16)
                y_ref[r0:r0 + n_rows, :] = jnp.dot(t, wd_ref[...], preferred_element_type=F32).astype(BF16)

            if r0 % MOE_CHUNK != 0:
                @pl.when(jnp.logical_and(c * MOE_CHUNK < cnt, r0 >= cnt))
                def _(r0=r0, n_rows=n_rows):
                    y_ref[r0:r0 + n_rows, :] = jnp.zeros((n_rows, D_MODEL), BF16)

        @pl.when(c * MOE_CHUNK < cnt)
        def _():
            cols = (lax.broadcasted_iota(jnp.int32, (tb, MOE_CHUNK), 1) + c * MOE_CHUNK).astype(F32)
            scatter = jnp.where(rank_col == cols, gate_col, 0.0).astype(BF16)
            o_ref[...] += jnp.dot(scatter, y_ref[c * MOE_CHUNK:(c + 1) * MOE_CHUNK, :],
                                  preferred_element_type=F32)


def _moe(xf, g, r_hi, r_lo, w_gate, w_up, w_down, tb):
    n = xf.shape[0]
    assert tb % MOE_CHUNK == 0
    return pl.pallas_call(
        _moe_kernel,
        grid=(n // tb, N_EXPERTS),
        in_specs=[
            pl.BlockSpec((tb, D_MODEL), lambda i, e: (i, 0)),
            pl.BlockSpec((1, D_MODEL), lambda i, e: (0, 0)),
            pl.BlockSpec((D_MODEL, LANES), lambda i, e: (0, 0)),
            pl.BlockSpec((D_MODEL, LANES), lambda i, e: (0, 0)),
            pl.BlockSpec((None, D_MODEL, D_FF_EXPERT), lambda i, e: (e, 0, 0)),
            pl.BlockSpec((None, D_MODEL, D_FF_EXPERT), lambda i, e: (e, 0, 0)),
            pl.BlockSpec((None, D_FF_EXPERT, D_MODEL), lambda i, e: (e, 0, 0)),
        ],
        out_specs=pl.BlockSpec((tb, D_MODEL), lambda i, e: (i, 0)),
        out_shape=jax.ShapeDtypeStruct((n, D_MODEL), F32),
        scratch_shapes=[
            pltpu.VMEM((tb, D_MODEL), BF16),
            pltpu.VMEM((tb, LANES), F32),
            pltpu.VMEM((tb, LANES), F32),
            pltpu.VMEM((LANES, tb), F32),
            pltpu.VMEM((1, LANES), F32),
            pltpu.VMEM((tb, D_MODEL), BF16),
        ],
        compiler_params=pltpu.CompilerParams(dimension_semantics=("arbitrary", "arbitrary"),
                                             vmem_limit_bytes=VMEM_LIMIT),
        name="moe",
    )(xf, g, r_hi, r_lo, w_gate, w_up, w_down)


def _pad_lanes(v, offset):
    return jnp.zeros((1, LANES), F32).at[0, offset:offset + v.shape[0]].set(v.astype(F32))


def _block_diag_ones(width, group):
    idx = np.arange(width) // group
    return jnp.asarray((idx[:, None] == idx[None, :]).astype(np.float32), BF16)


def _tile(n, pref):
    return pref if n % pref == 0 else n


def _mixer(xf, batch, seq, norm1, w_in, dn_conv, dn_a_log, dn_dt_bias, dn_onorm, fx_qnorm, fx_knorm,
           fx_f_bias, pool_w, pool_scale, w_out):
    n = batch * seq
    w_big, w_small = _winprep(w_in)

    dn_qkv, dn_z, fx_qkv, pool_x, small = _inproj(
        xf, norm1.reshape(1, D_MODEL), w_big, w_small, _tile(n, 512))

    bd = _block_diag_ones(DN_WIDTH, HEAD_DIM)
    gpar = jnp.concatenate([_pad_lanes(dn_a_log, LANE_A), _pad_lanes(dn_dt_bias, LANE_A)], axis=0)
    o_dn = _deltanet(dn_qkv, dn_z, small, dn_conv, gpar, jnp.tile(dn_onorm, DN_HEADS).reshape(1, DN_WIDTH),
                     bd, batch, seq, _tile(seq, 256))

    qg = (jnp.tile(fx_qnorm, FX_HEADS) * (HEAD_DIM ** -0.5 * LOG2E)).reshape(1, FX_WIDTH)
    kg = jnp.tile(fx_knorm, FX_HEADS).reshape(1, FX_WIDTH)
    qa, ka = _fxprep(fx_qkv, small, _pad_lanes(fx_f_bias, LANE_F), qg, kg, bd, batch, seq, _tile(seq, 512))
    o_fx = _fxattn(qa, ka, fx_qkv, batch, seq, _tile(seq, 512), _tile(seq, 512), 2)

    pool_bd = jax.scipy.linalg.block_diag(*[pool_w[gi] for gi in range(POOL_GROUPS)]).astype(BF16)
    return _outproj(xf, o_dn, o_fx, pool_x, w_out.astype(BF16), pool_bd,
                    pool_scale.reshape(1, POOL_WIDTH), batch, seq, _tile(seq, 512))


def kernel(x, norm1, w_in, dn_conv, dn_a_log, dn_dt_bias, dn_onorm, fx_qnorm, fx_knorm, fx_f_bias, pool_w,
           pool_scale, w_out, norm2, ffn_gate, ffn_up, ffn_down, router, moe_gate, moe_up, moe_down):
    batch, seq, _ = x.shape
    n = batch * seq
    xf = x.reshape(n, D_MODEL)
    depth = norm1.shape[0]
    for layer in range(depth):
        xf = _mixer(xf, batch, seq, norm1[layer], w_in[layer], dn_conv[layer], dn_a_log[layer],
                    dn_dt_bias[layer], dn_onorm[layer], fx_qnorm[layer], fx_knorm[layer],
                    fx_f_bias[layer], pool_w[layer], pool_scale[layer], w_out[layer])
        g2 = norm2[layer].reshape(1, D_MODEL)
        j = layer // 2
        if layer % 2 == 0:
            xf = _ffn(xf, g2, ffn_gate[j].astype(BF16), ffn_up[j].astype(BF16), ffn_down[j].astype(BF16),
                      _tile(n, 512), D_FF // 2)
        else:
            r = jnp.concatenate([router[j], jnp.zeros((D_MODEL, LANES - N_EXPERTS), F32)], axis=1)
            r_hi = r.astype(BF16)
            r_lo = (r - r_hi.astype(F32)).astype(BF16)
            xf = _moe(xf, g2, r_hi, r_lo, moe_gate[j].astype(BF16), moe_up[j].astype(BF16),
                      moe_down[j].astype(BF16), _tile(n, 1024))
    return xf.reshape(batch, seq, D_MODEL)
```

```python
import functools

import jax
import jax.numpy as jnp
import numpy as np
from jax import lax
from jax.experimental import pallas as pl
from jax.experimental.pallas import tpu as pltpu

F32 = jnp.float32
BF16 = jnp.bfloat16

D_MODEL = 1024
HEAD_DIM = 64
DN_HEADS = 6
DN_WIDTH = DN_HEADS * HEAD_DIM
FX_HEADS = 6
FX_WIDTH = FX_HEADS * HEAD_DIM
POOL_GROUPS = 4
POOL_GROUP_DIM = 64
POOL_WIDTH = POOL_GROUPS * POOL_GROUP_DIM
POOL_WINDOWS = (2, 4, 8, 16)
CONV_WIDTH = 4
DN_CHUNK = 64
D_FF = 2816
N_EXPERTS = 8
D_FF_EXPERT = 1536
EPS = 1e-6
LANES = 128
NEG_BIG = -1e30
LOG2E = 1.4426950408889634

QKV_W = 3 * DN_WIDTH
COL_DN_QKV = 0
COL_DN_Z = COL_DN_QKV + QKV_W
COL_FX_QKV = COL_DN_Z + DN_WIDTH
COL_POOL = COL_FX_QKV + QKV_W
N_BIG = COL_POOL + POOL_WIDTH
LANE_A = 0
LANE_B = DN_HEADS
LANE_F = 2 * DN_HEADS

VMEM_LIMIT = 56 * 1024 * 1024


def _mm(a, b):
    return jnp.dot(a.astype(BF16), b.astype(BF16), preferred_element_type=F32)


def _mm_nt(a, b):
    return lax.dot_general(a.astype(BF16), b.astype(BF16), (((1,), (1,)), ((), ())),
                           preferred_element_type=F32)


def _mm_tn(a, b):
    return lax.dot_general(a.astype(BF16), b.astype(BF16), (((0,), (0,)), ((), ())),
                           preferred_element_type=F32)


def _split_mm(a, b_bf16):
    hi = a.astype(BF16)
    lo = (a - hi.astype(F32)).astype(BF16)
    return (jnp.dot(hi, b_bf16, preferred_element_type=F32)
            + jnp.dot(lo, b_bf16, preferred_element_type=F32))


def _sigmoid(x):
    return 1.0 / (1.0 + jnp.exp(-x))


def _softplus(x):
    return jnp.maximum(x, 0.0) + jnp.log(1.0 + jnp.exp(-jnp.abs(x)))


def _cumsum_rows(x, period):
    ridx = lax.broadcasted_iota(jnp.int32, x.shape, 0) & (period - 1)
    s = 1
    while s < period:
        x = x + jnp.where(ridx >= s, pltpu.roll(x, s, axis=0), 0.0)
        s *= 2
    return x


SRC_AB = 4 * DN_WIDTH
SRC_FX_QKV = SRC_AB + 2 * DN_HEADS
SRC_F = SRC_FX_QKV + QKV_W
SRC_POOL = SRC_F + FX_HEADS
N_IN = SRC_POOL + POOL_WIDTH


def _winprep_kernel(w_ref, big_ref, small_ref):
    big_ref[:, COL_DN_QKV:COL_FX_QKV] = w_ref[:, 0:SRC_AB].astype(BF16)
    big_ref[:, COL_FX_QKV:COL_POOL] = w_ref[:, SRC_FX_QKV:SRC_F].astype(BF16)
    big_ref[:, COL_POOL:N_BIG] = w_ref[:, SRC_POOL:N_IN].astype(BF16)
    lane = lax.broadcasted_iota(jnp.int32, small_ref.shape, 1)
    ab = w_ref[:, SRC_AB:SRC_AB + LANES]
    ff = w_ref[:, SRC_F - LANE_F:SRC_F - LANE_F + LANES]
    small_ref[...] = jnp.where(lane < LANE_F, ab, jnp.where(lane < LANE_F + FX_HEADS, ff, 0.0)).astype(BF16)


def _winprep(w_in):
    assert w_in.shape == (D_MODEL, N_IN) and (SRC_F - LANE_F) % LANES == 0
    rows = 256
    return pl.pallas_call(
        _winprep_kernel,
        grid=(D_MODEL // rows,),
        in_specs=[pl.BlockSpec((rows, N_IN), lambda i: (i, 0))],
        out_specs=(pl.BlockSpec((rows, N_BIG), lambda i: (i, 0)),
                   pl.BlockSpec((rows, LANES), lambda i: (i, 0))),
        out_shape=(jax.ShapeDtypeStruct((D_MODEL, N_BIG), BF16),
                   jax.ShapeDtypeStruct((D_MODEL, LANES), BF16)),
        compiler_params=pltpu.CompilerParams(dimension_semantics=("arbitrary",),
                                             vmem_limit_bytes=VMEM_LIMIT),
        name="winprep",
    )(w_in)


def _inproj_kernel(x_ref, g_ref, w_ref, ws_ref, dnqkv_ref, dnz_ref, fxqkv_ref, pool_ref, small_ref):
    x = x_ref[...]
    ms = jnp.mean(x * x, axis=-1, keepdims=True)
    h = (x * lax.rsqrt(ms + EPS) * g_ref[...]).astype(BF16)

    def proj(lo, hi):
        return jnp.dot(h, w_ref[:, lo:hi], preferred_element_type=F32)

    dnqkv_ref[...] = proj(COL_DN_QKV, COL_DN_Z).astype(BF16)
    dnz_ref[...] = proj(COL_DN_Z, COL_FX_QKV).astype(BF16)
    fxqkv_ref[...] = proj(COL_FX_QKV, COL_POOL).astype(BF16)
    pool_ref[...] = proj(COL_POOL, N_BIG).astype(BF16)
    small_ref[...] = jnp.dot(h, ws_ref[...], preferred_element_type=F32)


def _inproj(xf, g, w_big, w_small, tm):
    n = xf.shape[0]
    out_shape = (
        jax.ShapeDtypeStruct((n, QKV_W), BF16),
        jax.ShapeDtypeStruct((n, DN_WIDTH), BF16),
        jax.ShapeDtypeStruct((n, QKV_W), BF16),
        jax.ShapeDtypeStruct((n, POOL_WIDTH), BF16),
        jax.ShapeDtypeStruct((n, LANES), F32),
    )
    row = lambda w: pl.BlockSpec((tm, w), lambda i: (i, 0))
    full = lambda a: pl.BlockSpec(a.shape, lambda i: (0,) * a.ndim)
    return pl.pallas_call(
        _inproj_kernel,
        grid=(n // tm,),
        in_specs=[row(D_MODEL), full(g), full(w_big), full(w_small)],
        out_specs=(row(QKV_W), row(DN_WIDTH), row(QKV_W), row(POOL_WIDTH), row(LANES)),
        out_shape=out_shape,
        compiler_params=pltpu.CompilerParams(dimension_semantics=("arbitrary",),
                                             vmem_limit_bytes=VMEM_LIMIT),
        name="inproj",
    )(xf, g, w_big, w_small)


def _dotf(a, b):
    return jnp.dot(a, b, preferred_element_type=F32)


def _unit_lower_inverses(a_list, masks_ref, eye):
    m8 = masks_ref[0]
    d = [a * m8 for a in a_list]
    db = [x.astype(BF16) for x in d]
    d2 = [_dotf(x, x).astype(BF16) for x in db]
    d4 = [_dotf(x, x).astype(BF16) for x in d2]
    x = [eye - dd for dd in d]
    x = [xx + _dotf(xx.astype(BF16), y) for xx, y in zip(x, d2)]
    x = [xx + _dotf(xx.astype(BF16), y) for xx, y in zip(x, d4)]
    for level in (1, 2, 3):
        me = masks_ref[level]
        e = [(a * me).astype(BF16) for a in a_list]
        xb = [xx.astype(BF16) for xx in x]
        ex = [_dotf(ee, xx).astype(BF16) for ee, xx in zip(e, xb)]
        x = [xx - _dotf(xxb, eex) for xx, xxb, eex in zip(x, xb, ex)]
    return x


def _dn_kernel(qkv_ref, z_ref, small_ref, convw_ref, gpar_ref, onorm_ref, bd_ref, ea_ref, eb_ref, masks_ref,
               o_ref, ext_ref, s_ref, oscr_ref, *, tc):
    i = pl.program_id(1)
    n_chunks = tc // DN_CHUNK
    heads = range(DN_HEADS)

    @pl.when(i == 0)
    def _():
        ext_ref[0:8, :] = jnp.zeros((8, QKV_W), F32)
        s_ref[...] = jnp.zeros_like(s_ref)

    x = qkv_ref[...].astype(F32)
    ext_ref[8:8 + tc, :] = x
    w = convw_ref[...]
    y = x * w[CONV_WIDTH - 1:CONV_WIDTH, :]
    for j in range(CONV_WIDTH - 1):
        y = y + ext_ref[pl.ds(8 - (CONV_WIDTH - 1) + j, tc), :] * w[j:j + 1, :]
    ext_ref[0:8, :] = x[tc - 8:tc, :]
    y = y * _sigmoid(y)

    bd = bd_ref[...]
    q = y[:, 0:DN_WIDTH]
    k = y[:, DN_WIDTH:2 * DN_WIDTH]
    v = y[:, 2 * DN_WIDTH:3 * DN_WIDTH]
    q = q * lax.rsqrt(_split_mm(q * q, bd) + EPS) * (HEAD_DIM ** -0.5)
    k = k * lax.rsqrt(_split_mm(k * k, bd) + EPS)

    sm = small_ref[...]
    gp = gpar_ref[...]
    g = -jnp.exp(gp[0:1, :]) * _softplus(sm + gp[1:2, :])
    beta = _sigmoid(sm)
    gcum = _cumsum_rows(g, DN_CHUNK)
    chunk_of_row = lax.broadcasted_iota(jnp.int32, (tc, LANES), 0) // DN_CHUNK
    glast = jnp.broadcast_to(gcum[tc - 1:tc, :], (tc, LANES))
    for c in range(n_chunks - 2, -1, -1):
        glast = jnp.where(chunk_of_row == c, gcum[(c + 1) * DN_CHUNK - 1:(c + 1) * DN_CHUNK, :], glast)

    ea = ea_ref[...]
    eb = eb_ref[...]
    beta_w = _split_mm(beta, eb)
    eg_w = _split_mm(jnp.exp(gcum), ea)
    ekd_w = _split_mm(jnp.exp(glast - gcum), ea)
    egl_w = _split_mm(jnp.exp(glast), ea)
    kb = k * beta_w
    vb = v * beta_w
    kbg = kb * eg_w
    q_dec = q * eg_w
    k_dec = k * ekd_w
    gcum_t = gcum.T

    def hs(a, h):
        return a[:, h * HEAD_DIM:(h + 1) * HEAD_DIM]

    r_i = lax.broadcasted_iota(jnp.int32, (tc, tc), 0)
    c_i = lax.broadcasted_iota(jnp.int32, (tc, tc), 1)
    same_chunk = (r_i // DN_CHUNK) == (c_i // DN_CHUNK)
    causal = jnp.logical_and(same_chunk, r_i >= c_i)
    eye = jnp.where(r_i == c_i, 1.0, 0.0)

    decay = []
    for h in heads:
        diff = gcum[:, LANE_A + h:LANE_A + h + 1] - gcum_t[LANE_A + h:LANE_A + h + 1, :]
        decay.append(jnp.where(causal, jnp.exp(jnp.where(causal, diff, 0.0)), 0.0))
    prod = [_mm_nt(jnp.concatenate([hs(q, h), hs(kb, h)], axis=0), hs(k, h)) for h in heads]
    qk = [(prod[h][0:tc] * decay[h]).astype(BF16) for h in heads]
    a_mat = [prod[h][tc:2 * tc] * decay[h] for h in heads]
    t_inv = _unit_lower_inverses(a_mat, masks_ref, eye)
    uw = [_dotf(t_inv[h].astype(BF16),
                jnp.concatenate([hs(vb, h), hs(kbg, h)], axis=1).astype(BF16)).astype(BF16) for h in heads]
    k_bd = [jnp.where(same_chunk, jnp.concatenate([hs(k_dec, h)] * n_chunks, axis=1), 0.0).astype(BF16)
            for h in heads]
    ktuw = [lax.dot_general(k_bd[h], uw[h], (((0,), (0,)), ((), ())), preferred_element_type=F32)
            for h in heads]
    qkuw = [_dotf(qk[h], uw[h]) for h in heads]

    for c in range(n_chunks):
        r0 = c * DN_CHUNK
        r1 = r0 + DN_CHUNK
        s_all = s_ref[...]
        lhs = [jnp.concatenate([hs(q_dec, h)[r0:r1] - qkuw[h][r0:r1, HEAD_DIM:2 * HEAD_DIM],
                                ktuw[h][r0:r1, HEAD_DIM:2 * HEAD_DIM]], axis=0) for h in heads]
        res = [_mm(lhs[h], hs(s_all, h)) for h in heads]
        oscr_ref[r0:r1, :] = jnp.concatenate(
            [res[h][0:DN_CHUNK] + qkuw[h][r0:r1, 0:HEAD_DIM] for h in heads], axis=1)
        s_ref[...] = s_all * egl_w[r0:r0 + 1, :] + jnp.concatenate(
            [ktuw[h][r0:r1, 0:HEAD_DIM] - res[h][DN_CHUNK:2 * DN_CHUNK] for h in heads], axis=1)

    o = oscr_ref[...]
    ms = _split_mm(o * o, bd) * (1.0 / HEAD_DIM)
    o = o * lax.rsqrt(ms + EPS) * onorm_ref[...]
    zz = z_ref[...].astype(F32)
    o_ref[...] = (o * (zz * _sigmoid(zz))).astype(BF16)


def _dn_constants(tc):
    ea = np.zeros((LANES, DN_WIDTH), np.float32)
    eb = np.zeros((LANES, DN_WIDTH), np.float32)
    for h in range(DN_HEADS):
        ea[LANE_A + h, h * HEAD_DIM:(h + 1) * HEAD_DIM] = 1.0
        eb[LANE_B + h, h * HEAD_DIM:(h + 1) * HEAD_DIM] = 1.0
    r = np.arange(tc)[:, None]
    c = np.arange(tc)[None, :]
    masks = [((r >> 3) == (c >> 3)) & (r != c)]
    for s in (3, 4, 5):
        masks.append(((r >> s) ^ (c >> s)) == 1)
    masks = np.stack(masks).astype(np.float32)
    return jnp.asarray(ea, BF16), jnp.asarray(eb, BF16), jnp.asarray(masks, F32)


def _deltanet(dn_qkv, dn_z, small, conv_w, gpar, onorm, bd, batch, seq, tc):
    n = batch * seq
    nt = seq // tc
    ea, eb, masks = _dn_constants(tc)
    row = lambda w: pl.BlockSpec((tc, w), lambda b, i: (b * nt + i, 0))
    full = lambda a: pl.BlockSpec(a.shape, lambda b, i: (0,) * a.ndim)
    return pl.pallas_call(
        functools.partial(_dn_kernel, tc=tc),
        grid=(batch, nt),
        in_specs=[row(QKV_W), row(DN_WIDTH), row(LANES), full(conv_w), full(gpar), full(onorm), full(bd),
                  full(ea), full(eb), full(masks)],
        out_specs=row(DN_WIDTH),
        out_shape=jax.ShapeDtypeStruct((n, DN_WIDTH), BF16),
        scratch_shapes=[
            pltpu.VMEM((tc + 8, QKV_W), F32),
            pltpu.VMEM((HEAD_DIM, DN_WIDTH), F32),
            pltpu.VMEM((tc, DN_WIDTH), F32),
        ],
        compiler_params=pltpu.CompilerParams(dimension_semantics=("arbitrary", "arbitrary"),
                                             vmem_limit_bytes=VMEM_LIMIT),
        name="deltanet",
    )(dn_qkv, dn_z, small, conv_w, gpar, onorm, bd, ea, eb, masks)


def _fxprep_kernel(qkv_ref, small_ref, fbias_ref, qg_ref, kg_ref, bd_ref, qa_ref, ka_ref, carry_ref, *, tc):
    i = pl.program_id(1)

    @pl.when(i == 0)
    def _():
        carry_ref[...] = jnp.zeros_like(carry_ref)

    bd = bd_ref[...]
    q = qkv_ref[:, 0:FX_WIDTH].astype(F32)
    k = qkv_ref[:, FX_WIDTH:2 * FX_WIDTH].astype(F32)
    q = q * lax.rsqrt(_split_mm(q * q, bd) * (1.0 / HEAD_DIM) + EPS) * qg_ref[...]
    k = k * lax.rsqrt(_split_mm(k * k, bd) * (1.0 / HEAD_DIM) + EPS) * kg_ref[...]

    logf = -_softplus(-(small_ref[...] + fbias_ref[...]))
    c = _cumsum_rows(logf, tc) + carry_ref[...]
    carry_ref[...] = c[tc - 1:tc, :]
    c = c * LOG2E

    c_hi = c.astype(BF16).astype(F32)
    r1 = c - c_hi
    c_mid = r1.astype(BF16).astype(F32)
    c_lo = (r1 - c_mid).astype(BF16).astype(F32)

    li = lax.broadcasted_iota(jnp.int32, (tc, HEAD_DIM), 1)
    for h in range(FX_HEADS):
        lane = LANE_F + h
        shape = (tc, HEAD_DIM)
        hi = jnp.broadcast_to(c_hi[:, lane:lane + 1], shape)
        mid = jnp.broadcast_to(c_mid[:, lane:lane + 1], shape)
        lo = jnp.broadcast_to(c_lo[:, lane:lane + 1], shape)
        q_ext = jnp.where(li == 0, hi, jnp.where(li == 1, mid, jnp.where(li == 2, lo,
                          jnp.where(li < 6, 1.0, 0.0))))
        k_ext = jnp.where(li < 3, 1.0, jnp.where(li == 3, -hi, jnp.where(li == 4, -mid,
                          jnp.where(li == 5, -lo, 0.0))))
        c0 = h * HEAD_DIM
        qa_ref[:, h * LANES:(h + 1) * LANES] = jnp.concatenate(
            [q[:, c0:c0 + HEAD_DIM], q_ext], axis=1).astype(BF16)
        ka_ref[:, h * LANES:(h + 1) * LANES] = jnp.concatenate(
            [k[:, c0:c0 + HEAD_DIM], k_ext], axis=1).astype(BF16)


def _fxprep(fx_qkv, small, fbias, qg, kg, bd, batch, seq, tc):
    n = batch * seq
    nt = seq // tc
    row = lambda w: pl.BlockSpec((tc, w), lambda b, i: (b * nt + i, 0))
    full = lambda a: pl.BlockSpec(a.shape, lambda b, i: (0,) * a.ndim)
    out = jax.ShapeDtypeStruct((n, FX_HEADS * LANES), BF16)
    return pl.pallas_call(
        functools.partial(_fxprep_kernel, tc=tc),
        grid=(batch, nt),
        in_specs=[row(QKV_W), row(LANES), full(fbias), full(qg), full(kg), full(bd)],
        out_specs=(row(FX_HEADS * LANES), row(FX_HEADS * LANES)),
        out_shape=(out, out),
        scratch_shapes=[pltpu.VMEM((1, LANES), F32)],
        compiler_params=pltpu.CompilerParams(dimension_semantics=("arbitrary", "arbitrary"),
                                             vmem_limit_bytes=VMEM_LIMIT),
        name="fxprep",
    )(fx_qkv, small, fbias, qg, kg, bd)


def _fxattn_kernel(qa_ref, ka_ref, v_ref, o_ref, *, seq, tq, tk, nsplit):
    w = tq // nsplit
    chains = [(hh, qs) for hh in range(2) for qs in range(nsplit)]
    blocks = [(qi, j) for qi in range(seq // tq) for j in range((qi + 1) * (tq // tk))]
    last_of_tile = {qi: (qi + 1) * (tq // tk) - 1 for qi in range(seq // tq)}
    state = {}
    scores = {}
    probs = {}

    def stage_scores(g):
        qi, j = blocks[g]
        q0, k0 = qi * tq, j * tk
        out = []
        for hh, qs in chains:
            first_q = q0 + qs * w
            n_keys = min(tk, first_q + w - k0)
            qa = qa_ref[first_q:first_q + w, hh * LANES:(hh + 1) * LANES]
            ka = ka_ref[k0:k0 + n_keys, hh * LANES:(hh + 1) * LANES]
            sc = lax.dot_general(ka, qa, (((1,), (1,)), ((), ())), preferred_element_type=F32)
            if k0 + n_keys - 1 > first_q:
                row = lax.broadcasted_iota(jnp.int32, sc.shape, 0)
                col = lax.broadcasted_iota(jnp.int32, sc.shape, 1)
                sc = jnp.where(row + (k0 - first_q) <= col, sc, NEG_BIG)
            out.append(sc)
        scores[g] = out

    def stage_softmax(g):
        qi, j = blocks[g]
        if j == 0:
            state[qi] = [(jnp.full((1, w), NEG_BIG, F32), jnp.zeros((1, w), F32),
                          jnp.zeros((HEAD_DIM, w), F32)) for _ in chains]
        out = []
        for ci in range(len(chains)):
            m_prev, l_prev, acc = state[qi][ci]
            sc = scores[g][ci]
            m_new = jnp.maximum(m_prev, jnp.max(sc, axis=0, keepdims=True))
            a = jnp.exp2(m_prev - m_new)
            p = jnp.exp2(sc - m_new)
            state[qi][ci] = (m_new, a * l_prev + jnp.sum(p, axis=0, keepdims=True), acc)
            out.append((p.astype(BF16), a))
        del scores[g]
        probs[g] = out

    def stage_values(g):
        qi, j = blocks[g]
        k0 = j * tk
        for ci, (hh, qs) in enumerate(chains):
            p, a = probs[g][ci]
            m_cur, l_cur, acc = state[qi][ci]
            vh = v_ref[k0:k0 + p.shape[0], hh * HEAD_DIM:(hh + 1) * HEAD_DIM]
            pv = lax.dot_general(vh, p, (((0,), (0,)), ((), ())), preferred_element_type=F32)
            state[qi][ci] = (m_cur, l_cur, a * acc + pv)
        del probs[g]
        if j == last_of_tile[qi]:
            cols = []
            for hh in range(2):
                per_q = [state[qi][hh * nsplit + qs] for qs in range(nsplit)]
                o_t = jnp.concatenate([acc / l_cur for (_, l_cur, acc) in per_q], axis=1)
                cols.append(o_t.T)
            o_ref[qi * tq:(qi + 1) * tq, :] = jnp.concatenate(cols, axis=1).astype(BF16)
            del state[qi]

    n_blocks = len(blocks)
    for t in range(n_blocks + 2):
        if t < n_blocks:
            stage_scores(t)
        if 0 <= t - 1 < n_blocks:
            stage_softmax(t - 1)
        if 0 <= t - 2 < n_blocks:
            stage_values(t - 2)


def _fxattn(qa, ka, fx_qkv, batch, seq, tq, tk, nsplit):
    n = batch * seq
    v_col0 = (2 * FX_WIDTH) // LANES
    return pl.pallas_call(
        functools.partial(_fxattn_kernel, seq=seq, tq=tq, tk=tk, nsplit=nsplit),
        grid=(batch, FX_HEADS // 2),
        in_specs=[
            pl.BlockSpec((seq, 2 * LANES), lambda b, hp: (b, hp)),
            pl.BlockSpec((seq, 2 * LANES), lambda b, hp: (b, hp)),
            pl.BlockSpec((seq, LANES), lambda b, hp: (b, v_col0 + hp)),
        ],
        out_specs=pl.BlockSpec((seq, LANES), lambda b, hp: (b, hp)),
        out_shape=jax.ShapeDtypeStruct((n, FX_WIDTH), BF16),
        compiler_params=pltpu.CompilerParams(dimension_semantics=("arbitrary", "arbitrary"),
                                             vmem_limit_bytes=VMEM_LIMIT),
        name="fxattn",
    )(qa, ka, fx_qkv)


def _outproj_kernel(x_ref, odn_ref, ofx_ref, px_ref, wout_ref, pbd_ref, pscale_ref, xo_ref, ext_ref, *, tc):
    i = pl.program_id(1)
    wmax = POOL_WINDOWS[-1]

    @pl.when(i == 0)
    def _():
        ext_ref[0:wmax, :] = jnp.zeros((wmax, POOL_WIDTH), F32)

    xp = px_ref[...].astype(F32)
    ext_ref[wmax:wmax + tc, :] = xp
    acc = xp
    sums = {}
    for j in range(1, wmax):
        acc = acc + ext_ref[pl.ds(wmax - j, tc), :]
        if j + 1 in POOL_WINDOWS:
            sums[j + 1] = acc
    ext_ref[0:wmax, :] = xp[tc - wmax:tc, :]

    grp = lax.broadcasted_iota(jnp.int32, (tc, POOL_WIDTH), 1) // POOL_GROUP_DIM
    t1 = lax.broadcasted_iota(jnp.int32, (tc, POOL_WIDTH), 0) + (i * tc + 1)
    wsum = sums[POOL_WINDOWS[-1]]
    wlen = jnp.full((tc, POOL_WIDTH), POOL_WINDOWS[-1], jnp.int32)
    for gi in range(POOL_GROUPS - 2, -1, -1):
        wsum = jnp.where(grp == gi, sums[POOL_WINDOWS[gi]], wsum)
        wlen = jnp.where(grp == gi, POOL_WINDOWS[gi], wlen)
    count = jnp.minimum(t1, wlen).astype(F32)
    y = wsum / count - xp
    pooled = _mm(y, pbd_ref[...]) * pscale_ref[...]

    out = x_ref[...]
    out = out + jnp.dot(odn_ref[...], wout_ref[0:DN_WIDTH, :], preferred_element_type=F32)
    out = out + jnp.dot(ofx_ref[...], wout_ref[DN_WIDTH:DN_WIDTH + FX_WIDTH, :], preferred_element_type=F32)
    out = out + jnp.dot(pooled.astype(BF16), wout_ref[DN_WIDTH + FX_WIDTH:D_MODEL, :],
                        preferred_element_type=F32)
    xo_ref[...] = out


def _outproj(xf, o_dn, o_fx, pool_x, w_out, pool_bd, pool_scale, batch, seq, tc):
    n = batch * seq
    nt = seq // tc
    row = lambda w: pl.BlockSpec((tc, w), lambda b, i: (b * nt + i, 0))
    full = lambda a: pl.BlockSpec(a.shape, lambda b, i: (0,) * a.ndim)
    return pl.pallas_call(
        functools.partial(_outproj_kernel, tc=tc),
        grid=(batch, nt),
        in_specs=[row(D_MODEL), row(DN_WIDTH), row(FX_WIDTH), row(POOL_WIDTH),
                  full(w_out), full(pool_bd), full(pool_scale)],
        out_specs=row(D_MODEL),
        out_shape=jax.ShapeDtypeStruct((n, D_MODEL), F32),
        scratch_shapes=[pltpu.VMEM((tc + POOL_WINDOWS[-1], POOL_WIDTH), F32)],
        compiler_params=pltpu.CompilerParams(dimension_semantics=("arbitrary", "arbitrary"),
                                             vmem_limit_bytes=VMEM_LIMIT),
        name="outproj",
    )(xf, o_dn, o_fx, pool_x, w_out, pool_bd, pool_scale)


def _ffn_kernel(x_ref, g_ref, wg_ref, wu_ref, wd_ref, o_ref, h_ref):
    f = pl.program_id(1)

    @pl.when(f == 0)
    def _():
        x = x_ref[...]
        ms = jnp.mean(x * x, axis=-1, keepdims=True)
        h_ref[...] = (x * lax.rsqrt(ms + EPS) * g_ref[...]).astype(BF16)
        o_ref[...] = x

    h = h_ref[...]
    a = jnp.dot(h, wg_ref[...], preferred_element_type=F32)
    u = jnp.dot(h, wu_ref[...], preferred_element_type=F32)
    t = (a * _sigmoid(a) * u).astype(BF16)
    o_ref[...] += jnp.dot(t, wd_ref[...], preferred_element_type=F32)


def _ffn(xf, g, w_gate, w_up, w_down, tm, tf):
    n = xf.shape[0]
    nf = D_FF // tf
    return pl.pallas_call(
        _ffn_kernel,
        grid=(n // tm, nf),
        in_specs=[
            pl.BlockSpec((tm, D_MODEL), lambda i, f: (i, 0)),
            pl.BlockSpec((1, D_MODEL), lambda i, f: (0, 0)),
            pl.BlockSpec((D_MODEL, tf), lambda i, f: (0, f)),
            pl.BlockSpec((D_MODEL, tf), lambda i, f: (0, f)),
            pl.BlockSpec((tf, D_MODEL), lambda i, f: (f, 0)),
        ],
        out_specs=pl.BlockSpec((tm, D_MODEL), lambda i, f: (i, 0)),
        out_shape=jax.ShapeDtypeStruct((n, D_MODEL), F32),
        scratch_shapes=[pltpu.VMEM((tm, D_MODEL), BF16)],
        compiler_params=pltpu.CompilerParams(dimension_semantics=("arbitrary", "arbitrary"),
                                             vmem_limit_bytes=VMEM_LIMIT),
        name="ffn",
    )(xf, g, w_gate, w_up, w_down)


MOE_SUB = 128
MOE_CHUNK = 256


def _moe_subtiles(tb):
    assert tb % MOE_CHUNK == 0 and MOE_CHUNK % MOE_SUB == 0
    return [(r, MOE_SUB) for r in range(0, tb, MOE_SUB)]


def _moe_kernel(x_ref, g_ref, rhi_ref, rlo_ref, wg_ref, wu_ref, wd_ref, o_ref,
                h_ref, gate_ref, rank_ref, rankt_ref, cnt_ref, y_ref):
    e = pl.program_id(1)
    tb = x_ref.shape[0]
    lane = lax.broadcasted_iota(jnp.int32, (tb, LANES), 1)

    @pl.when(e == 0)
    def _():
        x = x_ref[...]
        ms = jnp.mean(x * x, axis=-1, keepdims=True)
        hf = x * lax.rsqrt(ms + EPS) * g_ref[...]
        h_hi = hf.astype(BF16)
        h_lo = (hf - h_hi.astype(F32)).astype(BF16)
        h_ref[...] = h_hi
        o_ref[...] = x
        logits = (jnp.dot(h_hi, rhi_ref[...], preferred_element_type=F32)
                  + jnp.dot(h_lo, rhi_ref[...], preferred_element_type=F32)
                  + jnp.dot(h_hi, rlo_ref[...], preferred_element_type=F32))
        logits = jnp.where(lane < N_EXPERTS, logits, NEG_BIG)
        ex = jnp.exp(logits - jnp.max(logits, axis=-1, keepdims=True))
        probs = ex / jnp.sum(ex, axis=-1, keepdims=True)
        p1 = jnp.max(probs, axis=-1, keepdims=True)
        i1 = jnp.min(jnp.where(probs == p1, lane, LANES), axis=-1, keepdims=True)
        rest = jnp.where(lane == i1, -1.0, probs)
        p2 = jnp.max(rest, axis=-1, keepdims=True)
        i2 = jnp.min(jnp.where(rest == p2, lane, LANES), axis=-1, keepdims=True)
        denom = p1 + p2
        gate_ref[...] = (jnp.where(lane == i1, p1 / denom, 0.0)
                         + jnp.where(lane == i2, p2 / denom, 0.0))
        sel = jnp.where(jnp.logical_or(lane == i1, lane == i2), 1.0, 0.0)
        cum = _cumsum_rows(sel, tb)
        rank = jnp.where(sel > 0.0, cum - sel, -1.0)
        rank_ref[...] = rank
        rankt_ref[...] = rank.T
        cnt_ref[...] = cum[tb - 1:tb, :]

    cnt = jnp.sum(jnp.where(lane[0:1, :] == e, cnt_ref[...], 0.0)).astype(jnp.int32)
    sub_i = lax.broadcasted_iota(jnp.int32, (LANES, tb), 0)
    rank_row = jnp.sum(jnp.where(sub_i == e, rankt_ref[...], 0.0), axis=0, keepdims=True)
    rank_col = jnp.sum(jnp.where(lane == e, rank_ref[...], 0.0), axis=-1, keepdims=True)
    gate_col = jnp.sum(jnp.where(lane == e, gate_ref[...], 0.0), axis=-1, keepdims=True)

    subs = _moe_subtiles(tb)
    for c in range(tb // MOE_CHUNK):
        for r0, n_rows in [sub for sub in subs if sub[0] // MOE_CHUNK == c]:

            @pl.when(r0 < cnt)
            def _(r0=r0, n_rows=n_rows):
                rows = (lax.broadcasted_iota(jnp.int32, (n_rows, tb), 0) + r0).astype(F32)
                onehot = jnp.where(rank_row == rows, 1.0, 0.0).astype(BF16)
                xe = jnp.dot(onehot, h_ref[...], preferred_element_type=F32).astype(BF16)
                a = jnp.dot(xe, wg_ref[...], preferred_element_type=F32)
                u = jnp.dot(xe, wu_ref[...], preferred_element_type=F32)
                t = (a * _sigmoid(a) * u).astype(BF16)
                y_ref[r0:r0 + n_rows, :] = jnp.dot(t, wd_ref[...], preferred_element_type=F32).astype(BF16)

            if r0 % MOE_CHUNK != 0:
                @pl.when(jnp.logical_and(c * MOE_CHUNK < cnt, r0 >= cnt))
                def _(r0=r0, n_rows=n_rows):
                    y_ref[r0:r0 + n_rows, :] = jnp.zeros((n_rows, D_MODEL), BF16)

        @pl.when(c * MOE_CHUNK < cnt)
        def _():
            cols = (lax.broadcasted_iota(jnp.int32, (tb, MOE_CHUNK), 1) + c * MOE_CHUNK).astype(F32)
            scatter = jnp.where(rank_col == cols, gate_col, 0.0).astype(BF16)
            o_ref[...] += jnp.dot(scatter, y_ref[c * MOE_CHUNK:(c + 1) * MOE_CHUNK, :],
                                  preferred_element_type=F32)


def _moe(xf, g, r_hi, r_lo, w_gate, w_up, w_down, tb):
    n = xf.shape[0]
    assert tb % MOE_CHUNK == 0
    return pl.pallas_call(
        _moe_kernel,
        grid=(n // tb, N_EXPERTS),
        in_specs=[
            pl.BlockSpec((tb, D_MODEL), lambda i, e: (i, 0)),
            pl.BlockSpec((1, D_MODEL), lambda i, e: (0, 0)),
            pl.BlockSpec((D_MODEL, LANES), lambda i, e: (0, 0)),
            pl.BlockSpec((D_MODEL, LANES), lambda i, e: (0, 0)),
            pl.BlockSpec((None, D_MODEL, D_FF_EXPERT), lambda i, e: (e, 0, 0)),
            pl.BlockSpec((None, D_MODEL, D_FF_EXPERT), lambda i, e: (e, 0, 0)),
            pl.BlockSpec((None, D_FF_EXPERT, D_MODEL), lambda i, e: (e, 0, 0)),
        ],
        out_specs=pl.BlockSpec((tb, D_MODEL), lambda i, e: (i, 0)),
        out_shape=jax.ShapeDtypeStruct((n, D_MODEL), F32),
        scratch_shapes=[
            pltpu.VMEM((tb, D_MODEL), BF16),
            pltpu.VMEM((tb, LANES), F32),
            pltpu.VMEM((tb, LANES), F32),
            pltpu.VMEM((LANES, tb), F32),
            pltpu.VMEM((1, LANES), F32),
            pltpu.VMEM((tb, D_MODEL), BF16),
        ],
        compiler_params=pltpu.CompilerParams(dimension_semantics=("arbitrary", "arbitrary"),
                                             vmem_limit_bytes=VMEM_LIMIT),
        name="moe",
    )(xf, g, r_hi, r_lo, w_gate, w_up, w_down)


def _pad_lanes(v, offset):
    return jnp.zeros((1, LANES), F32).at[0, offset:offset + v.shape[0]].set(v.astype(F32))


def _block_diag_ones(width, group):
    idx = np.arange(width) // group
    return jnp.asarray((idx[:, None] == idx[None, :]).astype(np.float32), BF16)


def _tile(n, pref):
    return pref if n % pref == 0 else n


def _mixer(xf, batch, seq, norm1, w_in, dn_conv, dn_a_log, dn_dt_bias, dn_onorm, fx_qnorm, fx_knorm,
           fx_f_bias, pool_w, pool_scale, w_out):
    n = batch * seq
    w_big, w_small = _winprep(w_in)

    dn_qkv, dn_z, fx_qkv, pool_x, small = _inproj(
        xf, norm1.reshape(1, D_MODEL), w_big, w_small, _tile(n, 512))

    bd = _block_diag_ones(DN_WIDTH, HEAD_DIM)
    gpar = jnp.concatenate([_pad_lanes(dn_a_log, LANE_A), _pad_lanes(dn_dt_bias, LANE_A)], axis=0)
    o_dn = _deltanet(dn_qkv, dn_z, small, dn_conv, gpar, jnp.tile(dn_onorm, DN_HEADS).reshape(1, DN_WIDTH),
                     bd, batch, seq, _tile(seq, 256))

    qg = (jnp.tile(fx_qnorm, FX_HEADS) * (HEAD_DIM ** -0.5 * LOG2E)).reshape(1, FX_WIDTH)
    kg = jnp.tile(fx_knorm, FX_HEADS).reshape(1, FX_WIDTH)
    qa, ka = _fxprep(fx_qkv, small, _pad_lanes(fx_f_bias, LANE_F), qg, kg, bd, batch, seq, _tile(seq, 512))
    o_fx = _fxattn(qa, ka, fx_qkv, batch, seq, _tile(seq, 512), _tile(seq, 512), 2)

    pool_bd = jax.scipy.linalg.block_diag(*[pool_w[gi] for gi in range(POOL_GROUPS)]).astype(BF16)
    return _outproj(xf, o_dn, o_fx, pool_x, w_out.astype(BF16), pool_bd,
                    pool_scale.reshape(1, POOL_WIDTH), batch, seq, _tile(seq, 512))


def kernel(x, norm1, w_in, dn_conv, dn_a_log, dn_dt_bias, dn_onorm, fx_qnorm, fx_knorm, fx_f_bias, pool_w,
           pool_scale, w_out, norm2, ffn_gate, ffn_up, ffn_down, router, moe_gate, moe_up, moe_down):
    batch, seq, _ = x.shape
    n = batch * seq
    xf = x.reshape(n, D_MODEL)
    depth = norm1.shape[0]
    for layer in range(depth):
        xf = _mixer(xf, batch, seq, norm1[layer], w_in[layer], dn_conv[layer], dn_a_log[layer],
                    dn_dt_bias[layer], dn_onorm[layer], fx_qnorm[layer], fx_knorm[layer],
                    fx_f_bias[layer], pool_w[layer], pool_scale[layer], w_out[layer])
        g2 = norm2[layer].reshape(1, D_MODEL)
        j = layer // 2
        if layer % 2 == 0:
            xf = _ffn(xf, g2, ffn_gate[j].astype(BF16), ffn_up[j].astype(BF16), ffn_down[j].astype(BF16),
                      _tile(n, 1024), D_FF // 2)
        else:
            r = jnp.concatenate([router[j], jnp.zeros((D_MODEL, LANES - N_EXPERTS), F32)], axis=1)
            r_hi = r.astype(BF16)
            r_lo = (r - r_hi.astype(F32)).astype(BF16)
            xf = _moe(xf, g2, r_hi, r_lo, moe_gate[j].astype(BF16), moe_up[j].astype(BF16),
                      moe_down[j].astype(BF16), _tile(n, 1024))
    return xf.reshape(batch, seq, D_MODEL)
```

```python
import functools

import jax
import jax.numpy as jnp
import numpy as np
from jax import lax
from jax.experimental import pallas as pl
from jax.experimental.pallas import tpu as pltpu

F32 = jnp.float32
BF16 = jnp.bfloat16

D_MODEL = 1024
HEAD_DIM = 64
DN_HEADS = 6
DN_WIDTH = DN_HEADS * HEAD_DIM
FX_HEADS = 6
FX_WIDTH = FX_HEADS * HEAD_DIM
POOL_GROUPS = 4
POOL_GROUP_DIM = 64
POOL_WIDTH = POOL_GROUPS * POOL_GROUP_DIM
POOL_WINDOWS = (2, 4, 8, 16)
CONV_WIDTH = 4
DN_CHUNK = 64
D_FF = 2816
N_EXPERTS = 8
D_FF_EXPERT = 1536
EPS = 1e-6
LANES = 128
NEG_BIG = -1e30
LOG2E = 1.4426950408889634

QKV_W = 3 * DN_WIDTH
COL_DN_QKV = 0
COL_DN_Z = COL_DN_QKV + QKV_W
COL_FX_QKV = COL_DN_Z + DN_WIDTH
COL_POOL = COL_FX_QKV + QKV_W
N_BIG = COL_POOL + POOL_WIDTH
LANE_A = 0
LANE_B = DN_HEADS
LANE_F = 2 * DN_HEADS

VMEM_LIMIT = 56 * 1024 * 1024


def _mm(a, b):
    return jnp.dot(a.astype(BF16), b.astype(BF16), preferred_element_type=F32)


def _mm_nt(a, b):
    return lax.dot_general(a.astype(BF16), b.astype(BF16), (((1,), (1,)), ((), ())),
                           preferred_element_type=F32)


def _mm_tn(a, b):
    return lax.dot_general(a.astype(BF16), b.astype(BF16), (((0,), (0,)), ((), ())),
                           preferred_element_type=F32)


def _split_mm(a, b_bf16):
    hi = a.astype(BF16)
    lo = (a - hi.astype(F32)).astype(BF16)
    return (jnp.dot(hi, b_bf16, preferred_element_type=F32)
            + jnp.dot(lo, b_bf16, preferred_element_type=F32))


def _sigmoid(x):
    return 1.0 / (1.0 + jnp.exp(-x))


def _softplus(x):
    return jnp.maximum(x, 0.0) + jnp.log(1.0 + jnp.exp(-jnp.abs(x)))


def _cumsum_rows(x, period):
    ridx = lax.broadcasted_iota(jnp.int32, x.shape, 0) & (period - 1)
    s = 1
    while s < period:
        x = x + jnp.where(ridx >= s, pltpu.roll(x, s, axis=0), 0.0)
        s *= 2
    return x


SRC_AB = 4 * DN_WIDTH
SRC_FX_QKV = SRC_AB + 2 * DN_HEADS
SRC_F = SRC_FX_QKV + QKV_W
SRC_POOL = SRC_F + FX_HEADS
N_IN = SRC_POOL + POOL_WIDTH


def _winprep_kernel(w_ref, big_ref, small_ref):
    big_ref[:, COL_DN_QKV:COL_FX_QKV] = w_ref[:, 0:SRC_AB].astype(BF16)
    big_ref[:, COL_FX_QKV:COL_POOL] = w_ref[:, SRC_FX_QKV:SRC_F].astype(BF16)
    big_ref[:, COL_POOL:N_BIG] = w_ref[:, SRC_POOL:N_IN].astype(BF16)
    lane = lax.broadcasted_iota(jnp.int32, small_ref.shape, 1)
    ab = w_ref[:, SRC_AB:SRC_AB + LANES]
    ff = w_ref[:, SRC_F - LANE_F:SRC_F - LANE_F + LANES]
    small_ref[...] = jnp.where(lane < LANE_F, ab, jnp.where(lane < LANE_F + FX_HEADS, ff, 0.0)).astype(BF16)


def _winprep(w_in):
    assert w_in.shape == (D_MODEL, N_IN) and (SRC_F - LANE_F) % LANES == 0
    rows = 256
    return pl.pallas_call(
        _winprep_kernel,
        grid=(D_MODEL // rows,),
        in_specs=[pl.BlockSpec((rows, N_IN), lambda i: (i, 0))],
        out_specs=(pl.BlockSpec((rows, N_BIG), lambda i: (i, 0)),
                   pl.BlockSpec((rows, LANES), lambda i: (i, 0))),
        out_shape=(jax.ShapeDtypeStruct((D_MODEL, N_BIG), BF16),
                   jax.ShapeDtypeStruct((D_MODEL, LANES), BF16)),
        compiler_params=pltpu.CompilerParams(dimension_semantics=("arbitrary",),
                                             vmem_limit_bytes=VMEM_LIMIT),
        name="winprep",
    )(w_in)


def _inproj_kernel(x_ref, g_ref, w_ref, ws_ref, dnqkv_ref, dnz_ref, fxqkv_ref, pool_ref, small_ref):
    x = x_ref[...]
    ms = jnp.mean(x * x, axis=-1, keepdims=True)
    h = (x * lax.rsqrt(ms + EPS) * g_ref[...]).astype(BF16)

    def proj(lo, hi):
        return jnp.dot(h, w_ref[:, lo:hi], preferred_element_type=F32)

    dnqkv_ref[...] = proj(COL_DN_QKV, COL_DN_Z).astype(BF16)
    dnz_ref[...] = proj(COL_DN_Z, COL_FX_QKV).astype(BF16)
    fxqkv_ref[...] = proj(COL_FX_QKV, COL_POOL).astype(BF16)
    pool_ref[...] = proj(COL_POOL, N_BIG).astype(BF16)
    small_ref[...] = jnp.dot(h, ws_ref[...], preferred_element_type=F32)


def _inproj(xf, g, w_big, w_small, tm):
    n = xf.shape[0]
    out_shape = (
        jax.ShapeDtypeStruct((n, QKV_W), BF16),
        jax.ShapeDtypeStruct((n, DN_WIDTH), BF16),
        jax.ShapeDtypeStruct((n, QKV_W), BF16),
        jax.ShapeDtypeStruct((n, POOL_WIDTH), BF16),
        jax.ShapeDtypeStruct((n, LANES), F32),
    )
    row = lambda w: pl.BlockSpec((tm, w), lambda i: (i, 0))
    full = lambda a: pl.BlockSpec(a.shape, lambda i: (0,) * a.ndim)
    return pl.pallas_call(
        _inproj_kernel,
        grid=(n // tm,),
        in_specs=[row(D_MODEL), full(g), full(w_big), full(w_small)],
        out_specs=(row(QKV_W), row(DN_WIDTH), row(QKV_W), row(POOL_WIDTH), row(LANES)),
        out_shape=out_shape,
        compiler_params=pltpu.CompilerParams(dimension_semantics=("arbitrary",),
                                             vmem_limit_bytes=VMEM_LIMIT),
        name="inproj",
    )(xf, g, w_big, w_small)


def _dotf(a, b):
    return jnp.dot(a, b, preferred_element_type=F32)


def _unit_lower_inverses(a_list, masks_ref, eye):
    m8 = masks_ref[0]
    d = [a * m8 for a in a_list]
    db = [x.astype(BF16) for x in d]
    d2 = [_dotf(x, x).astype(BF16) for x in db]
    d4 = [_dotf(x, x).astype(BF16) for x in d2]
    x = [eye - dd for dd in d]
    x = [xx + _dotf(xx.astype(BF16), y) for xx, y in zip(x, d2)]
    x = [xx + _dotf(xx.astype(BF16), y) for xx, y in zip(x, d4)]
    for level in (1, 2, 3):
        me = masks_ref[level]
        e = [(a * me).astype(BF16) for a in a_list]
        xb = [xx.astype(BF16) for xx in x]
        ex = [_dotf(ee, xx).astype(BF16) for ee, xx in zip(e, xb)]
        x = [xx - _dotf(xxb, eex) for xx, xxb, eex in zip(x, xb, ex)]
    return x


def _dn_kernel(qkv_ref, z_ref, small_ref, convw_ref, gpar_ref, onorm_ref, bd_ref, ea_ref, eb_ref, masks_ref,
               o_ref, ext_ref, s_ref, oscr_ref, *, tc):
    i = pl.program_id(1)
    n_chunks = tc // DN_CHUNK
    heads = range(DN_HEADS)

    @pl.when(i == 0)
    def _():
        ext_ref[0:8, :] = jnp.zeros((8, QKV_W), F32)
        s_ref[...] = jnp.zeros_like(s_ref)

    x = qkv_ref[...].astype(F32)
    ext_ref[8:8 + tc, :] = x
    w = convw_ref[...]
    y = x * w[CONV_WIDTH - 1:CONV_WIDTH, :]
    for j in range(CONV_WIDTH - 1):
        y = y + ext_ref[pl.ds(8 - (CONV_WIDTH - 1) + j, tc), :] * w[j:j + 1, :]
    ext_ref[0:8, :] = x[tc - 8:tc, :]
    y = y * _sigmoid(y)

    bd = bd_ref[...]
    q = y[:, 0:DN_WIDTH]
    k = y[:, DN_WIDTH:2 * DN_WIDTH]
    v = y[:, 2 * DN_WIDTH:3 * DN_WIDTH]
    q = q * lax.rsqrt(_split_mm(q * q, bd) + EPS) * (HEAD_DIM ** -0.5)
    k = k * lax.rsqrt(_split_mm(k * k, bd) + EPS)

    sm = small_ref[...]
    gp = gpar_ref[...]
    g = -jnp.exp(gp[0:1, :]) * _softplus(sm + gp[1:2, :])
    beta = _sigmoid(sm)
    gcum = _cumsum_rows(g, DN_CHUNK)
    chunk_of_row = lax.broadcasted_iota(jnp.int32, (tc, LANES), 0) // DN_CHUNK
    glast = jnp.broadcast_to(gcum[tc - 1:tc, :], (tc, LANES))
    for c in range(n_chunks - 2, -1, -1):
        glast = jnp.where(chunk_of_row == c, gcum[(c + 1) * DN_CHUNK - 1:(c + 1) * DN_CHUNK, :], glast)

    ea = ea_ref[...]
    eb = eb_ref[...]
    beta_w = _split_mm(beta, eb)
    eg_w = _split_mm(jnp.exp(gcum), ea)
    ekd_w = _split_mm(jnp.exp(glast - gcum), ea)
    egl_w = _split_mm(jnp.exp(glast), ea)
    kb = k * beta_w
    vb = v * beta_w
    kbg = kb * eg_w
    q_dec = q * eg_w
    k_dec = k * ekd_w
    gcum_t = gcum.T

    def hs(a, h):
        return a[:, h * HEAD_DIM:(h + 1) * HEAD_DIM]

    r_i = lax.broadcasted_iota(jnp.int32, (tc, tc), 0)
    c_i = lax.broadcasted_iota(jnp.int32, (tc, tc), 1)
    same_chunk = (r_i // DN_CHUNK) == (c_i // DN_CHUNK)
    causal = jnp.logical_and(same_chunk, r_i >= c_i)
    eye = jnp.where(r_i == c_i, 1.0, 0.0)

    decay = []
    for h in heads:
        diff = gcum[:, LANE_A + h:LANE_A + h + 1] - gcum_t[LANE_A + h:LANE_A + h + 1, :]
        decay.append(jnp.where(causal, jnp.exp(jnp.where(causal, diff, 0.0)), 0.0))
    prod = [_mm_nt(jnp.concatenate([hs(q, h), hs(kb, h)], axis=0), hs(k, h)) for h in heads]
    qk = [(prod[h][0:tc] * decay[h]).astype(BF16) for h in heads]
    a_mat = [prod[h][tc:2 * tc] * decay[h] for h in heads]
    t_inv = _unit_lower_inverses(a_mat, masks_ref, eye)
    uw = [_dotf(t_inv[h].astype(BF16),
                jnp.concatenate([hs(vb, h), hs(kbg, h)], axis=1).astype(BF16)).astype(BF16) for h in heads]
    k_bd = [jnp.where(same_chunk, jnp.concatenate([hs(k_dec, h)] * n_chunks, axis=1), 0.0).astype(BF16)
            for h in heads]
    ktuw = [lax.dot_general(k_bd[h], uw[h], (((0,), (0,)), ((), ())), preferred_element_type=F32)
            for h in heads]
    qkuw = [_dotf(qk[h], uw[h]) for h in heads]

    for c in range(n_chunks):
        r0 = c * DN_CHUNK
        r1 = r0 + DN_CHUNK
        s_all = s_ref[...]
        lhs = [jnp.concatenate([hs(q_dec, h)[r0:r1] - qkuw[h][r0:r1, HEAD_DIM:2 * HEAD_DIM],
                                ktuw[h][r0:r1, HEAD_DIM:2 * HEAD_DIM]], axis=0) for h in heads]
        res = [_mm(lhs[h], hs(s_all, h)) for h in heads]
        oscr_ref[r0:r1, :] = jnp.concatenate(
            [res[h][0:DN_CHUNK] + qkuw[h][r0:r1, 0:HEAD_DIM] for h in heads], axis=1)
        s_ref[...] = s_all * egl_w[r0:r0 + 1, :] + jnp.concatenate(
            [ktuw[h][r0:r1, 0:HEAD_DIM] - res[h][DN_CHUNK:2 * DN_CHUNK] for h in heads], axis=1)

    o = oscr_ref[...]
    ms = _split_mm(o * o, bd) * (1.0 / HEAD_DIM)
    o = o * lax.rsqrt(ms + EPS) * onorm_ref[...]
    zz = z_ref[...].astype(F32)
    o_ref[...] = (o * (zz * _sigmoid(zz))).astype(BF16)


def _dn_constants(tc):
    ea = np.zeros((LANES, DN_WIDTH), np.float32)
    eb = np.zeros((LANES, DN_WIDTH), np.float32)
    for h in range(DN_HEADS):
        ea[LANE_A + h, h * HEAD_DIM:(h + 1) * HEAD_DIM] = 1.0
        eb[LANE_B + h, h * HEAD_DIM:(h + 1) * HEAD_DIM] = 1.0
    r = np.arange(tc)[:, None]
    c = np.arange(tc)[None, :]
    masks = [((r >> 3) == (c >> 3)) & (r != c)]
    for s in (3, 4, 5):
        masks.append(((r >> s) ^ (c >> s)) == 1)
    masks = np.stack(masks).astype(np.float32)
    return jnp.asarray(ea, BF16), jnp.asarray(eb, BF16), jnp.asarray(masks, F32)


def _deltanet(dn_qkv, dn_z, small, conv_w, gpar, onorm, bd, batch, seq, tc):
    n = batch * seq
    nt = seq // tc
    ea, eb, masks = _dn_constants(tc)
    row = lambda w: pl.BlockSpec((tc, w), lambda b, i: (b * nt + i, 0))
    full = lambda a: pl.BlockSpec(a.shape, lambda b, i: (0,) * a.ndim)
    return pl.pallas_call(
        functools.partial(_dn_kernel, tc=tc),
        grid=(batch, nt),
        in_specs=[row(QKV_W), row(DN_WIDTH), row(LANES), full(conv_w), full(gpar), full(onorm), full(bd),
                  full(ea), full(eb), full(masks)],
        out_specs=row(DN_WIDTH),
        out_shape=jax.ShapeDtypeStruct((n, DN_WIDTH), BF16),
        scratch_shapes=[
            pltpu.VMEM((tc + 8, QKV_W), F32),
            pltpu.VMEM((HEAD_DIM, DN_WIDTH), F32),
            pltpu.VMEM((tc, DN_WIDTH), F32),
        ],
        compiler_params=pltpu.CompilerParams(dimension_semantics=("arbitrary", "arbitrary"),
                                             vmem_limit_bytes=VMEM_LIMIT),
        name="deltanet",
    )(dn_qkv, dn_z, small, conv_w, gpar, onorm, bd, ea, eb, masks)


def _fxprep_kernel(qkv_ref, small_ref, fbias_ref, qg_ref, kg_ref, bd_ref, qa_ref, ka_ref, carry_ref, *, tc):
    i = pl.program_id(1)

    @pl.when(i == 0)
    def _():
        carry_ref[...] = jnp.zeros_like(carry_ref)

    bd = bd_ref[...]
    q = qkv_ref[:, 0:FX_WIDTH].astype(F32)
    k = qkv_ref[:, FX_WIDTH:2 * FX_WIDTH].astype(F32)
    q = q * lax.rsqrt(_split_mm(q * q, bd) * (1.0 / HEAD_DIM) + EPS) * qg_ref[...]
    k = k * lax.rsqrt(_split_mm(k * k, bd) * (1.0 / HEAD_DIM) + EPS) * kg_ref[...]

    logf = -_softplus(-(small_ref[...] + fbias_ref[...]))
    c = _cumsum_rows(logf, tc) + carry_ref[...]
    carry_ref[...] = c[tc - 1:tc, :]
    c = c * LOG2E

    c_hi = c.astype(BF16).astype(F32)
    r1 = c - c_hi
    c_mid = r1.astype(BF16).astype(F32)
    c_lo = (r1 - c_mid).astype(BF16).astype(F32)

    li = lax.broadcasted_iota(jnp.int32, (tc, HEAD_DIM), 1)
    for h in range(FX_HEADS):
        lane = LANE_F + h
        shape = (tc, HEAD_DIM)
        hi = jnp.broadcast_to(c_hi[:, lane:lane + 1], shape)
        mid = jnp.broadcast_to(c_mid[:, lane:lane + 1], shape)
        lo = jnp.broadcast_to(c_lo[:, lane:lane + 1], shape)
        q_ext = jnp.where(li == 0, hi, jnp.where(li == 1, mid, jnp.where(li == 2, lo,
                          jnp.where(li < 6, 1.0, 0.0))))
        k_ext = jnp.where(li < 3, 1.0, jnp.where(li == 3, -hi, jnp.where(li == 4, -mid,
                          jnp.where(li == 5, -lo, 0.0))))
        c0 = h * HEAD_DIM
        qa_ref[:, h * LANES:(h + 1) * LANES] = jnp.concatenate(
            [q[:, c0:c0 + HEAD_DIM], q_ext], axis=1).astype(BF16)
        ka_ref[:, h * LANES:(h + 1) * LANES] = jnp.concatenate(
            [k[:, c0:c0 + HEAD_DIM], k_ext], axis=1).astype(BF16)


def _fxprep(fx_qkv, small, fbias, qg, kg, bd, batch, seq, tc):
    n = batch * seq
    nt = seq // tc
    row = lambda w: pl.BlockSpec((tc, w), lambda b, i: (b * nt + i, 0))
    full = lambda a: pl.BlockSpec(a.shape, lambda b, i: (0,) * a.ndim)
    out = jax.ShapeDtypeStruct((n, FX_HEADS * LANES), BF16)
    return pl.pallas_call(
        functools.partial(_fxprep_kernel, tc=tc),
        grid=(batch, nt),
        in_specs=[row(QKV_W), row(LANES), full(fbias), full(qg), full(kg), full(bd)],
        out_specs=(row(FX_HEADS * LANES), row(FX_HEADS * LANES)),
        out_shape=(out, out),
        scratch_shapes=[pltpu.VMEM((1, LANES), F32)],
        compiler_params=pltpu.CompilerParams(dimension_semantics=("arbitrary", "arbitrary"),
                                             vmem_limit_bytes=VMEM_LIMIT),
        name="fxprep",
    )(fx_qkv, small, fbias, qg, kg, bd)


def _fxattn_kernel(qa_ref, ka_ref, v_ref, o_ref, *, seq, tq, tk, nsplit):
    w = tq // nsplit
    chains = [(hh, qs) for hh in range(2) for qs in range(nsplit)]
    blocks = [(qi, j) for qi in range(seq // tq) for j in range((qi + 1) * (tq // tk))]
    last_of_tile = {qi: (qi + 1) * (tq // tk) - 1 for qi in range(seq // tq)}
    state = {}
    scores = {}
    probs = {}

    def stage_scores(g):
        qi, j = blocks[g]
        q0, k0 = qi * tq, j * tk
        out = []
        for hh, qs in chains:
            first_q = q0 + qs * w
            n_keys = min(tk, first_q + w - k0)
            qa = qa_ref[first_q:first_q + w, hh * LANES:(hh + 1) * LANES]
            ka = ka_ref[k0:k0 + n_keys, hh * LANES:(hh + 1) * LANES]
            sc = lax.dot_general(ka, qa, (((1,), (1,)), ((), ())), preferred_element_type=F32)
            if k0 + n_keys - 1 > first_q:
                row = lax.broadcasted_iota(jnp.int32, sc.shape, 0)
                col = lax.broadcasted_iota(jnp.int32, sc.shape, 1)
                sc = jnp.where(row + (k0 - first_q) <= col, sc, NEG_BIG)
            out.append(sc)
        scores[g] = out

    def stage_softmax(g):
        qi, j = blocks[g]
        if j == 0:
            state[qi] = [(jnp.full((1, w), NEG_BIG, F32), jnp.zeros((1, w), F32),
                          jnp.zeros((HEAD_DIM, w), F32)) for _ in chains]
        out = []
        for ci in range(len(chains)):
            m_prev, l_prev, acc = state[qi][ci]
            sc = scores[g][ci]
            m_new = jnp.maximum(m_prev, jnp.max(sc, axis=0, keepdims=True))
            a = jnp.exp2(m_prev - m_new)
            p = jnp.exp2(sc - m_new)
            state[qi][ci] = (m_new, a * l_prev + jnp.sum(p, axis=0, keepdims=True), acc)
            out.append((p.astype(BF16), a))
        del scores[g]
        probs[g] = out

    def stage_values(g):
        qi, j = blocks[g]
        k0 = j * tk
        for ci, (hh, qs) in enumerate(chains):
            p, a = probs[g][ci]
            m_cur, l_cur, acc = state[qi][ci]
            vh = v_ref[k0:k0 + p.shape[0], hh * HEAD_DIM:(hh + 1) * HEAD_DIM]
            pv = lax.dot_general(vh, p, (((0,), (0,)), ((), ())), preferred_element_type=F32)
            state[qi][ci] = (m_cur, l_cur, a * acc + pv)
        del probs[g]
        if j == last_of_tile[qi]:
            cols = []
            for hh in range(2):
                per_q = [state[qi][hh * nsplit + qs] for qs in range(nsplit)]
                o_t = jnp.concatenate([acc / l_cur for (_, l_cur, acc) in per_q], axis=1)
                cols.append(o_t.T)
            o_ref[qi * tq:(qi + 1) * tq, :] = jnp.concatenate(cols, axis=1).astype(BF16)
            del state[qi]

    n_blocks = len(blocks)
    for t in range(n_blocks + 2):
        if t < n_blocks:
            stage_scores(t)
        if 0 <= t - 1 < n_blocks:
            stage_softmax(t - 1)
        if 0 <= t - 2 < n_blocks:
            stage_values(t - 2)


def _fxattn(qa, ka, fx_qkv, batch, seq, tq, tk, nsplit):
    n = batch * seq
    v_col0 = (2 * FX_WIDTH) // LANES
    return pl.pallas_call(
        functools.partial(_fxattn_kernel, seq=seq, tq=tq, tk=tk, nsplit=nsplit),
        grid=(batch, FX_HEADS // 2),
        in_specs=[
            pl.BlockSpec((seq, 2 * LANES), lambda b, hp: (b, hp)),
            pl.BlockSpec((seq, 2 * LANES), lambda b, hp: (b, hp)),
            pl.BlockSpec((seq, LANES), lambda b, hp: (b, v_col0 + hp)),
        ],
        out_specs=pl.BlockSpec((seq, LANES), lambda b, hp: (b, hp)),
        out_shape=jax.ShapeDtypeStruct((n, FX_WIDTH), BF16),
        compiler_params=pltpu.CompilerParams(dimension_semantics=("arbitrary", "arbitrary"),
                                             vmem_limit_bytes=VMEM_LIMIT),
        name="fxattn",
    )(qa, ka, fx_qkv)


def _outproj_kernel(x_ref, odn_ref, ofx_ref, px_ref, wout_ref, pbd_ref, pscale_ref, xo_ref, ext_ref, *, tc):
    i = pl.program_id(1)
    wmax = POOL_WINDOWS[-1]

    @pl.when(i == 0)
    def _():
        ext_ref[0:wmax, :] = jnp.zeros((wmax, POOL_WIDTH), F32)

    xp = px_ref[...].astype(F32)
    ext_ref[wmax:wmax + tc, :] = xp
    acc = xp
    sums = {}
    for j in range(1, wmax):
        acc = acc + ext_ref[pl.ds(wmax - j, tc), :]
        if j + 1 in POOL_WINDOWS:
            sums[j + 1] = acc
    ext_ref[0:wmax, :] = xp[tc - wmax:tc, :]

    grp = lax.broadcasted_iota(jnp.int32, (tc, POOL_WIDTH), 1) // POOL_GROUP_DIM
    t1 = lax.broadcasted_iota(jnp.int32, (tc, POOL_WIDTH), 0) + (i * tc + 1)
    wsum = sums[POOL_WINDOWS[-1]]
    wlen = jnp.full((tc, POOL_WIDTH), POOL_WINDOWS[-1], jnp.int32)
    for gi in range(POOL_GROUPS - 2, -1, -1):
        wsum = jnp.where(grp == gi, sums[POOL_WINDOWS[gi]], wsum)
        wlen = jnp.where(grp == gi, POOL_WINDOWS[gi], wlen)
    count = jnp.minimum(t1, wlen).astype(F32)
    y = wsum / count - xp
    pooled = _mm(y, pbd_ref[...]) * pscale_ref[...]

    out = x_ref[...]
    out = out + jnp.dot(odn_ref[...], wout_ref[0:DN_WIDTH, :], preferred_element_type=F32)
    out = out + jnp.dot(ofx_ref[...], wout_ref[DN_WIDTH:DN_WIDTH + FX_WIDTH, :], preferred_element_type=F32)
    out = out + jnp.dot(pooled.astype(BF16), wout_ref[DN_WIDTH + FX_WIDTH:D_MODEL, :],
                        preferred_element_type=F32)
    xo_ref[...] = out


def _outproj(xf, o_dn, o_fx, pool_x, w_out, pool_bd, pool_scale, batch, seq, tc):
    n = batch * seq
    nt = seq // tc
    row = lambda w: pl.BlockSpec((tc, w), lambda b, i: (b * nt + i, 0))
    full = lambda a: pl.BlockSpec(a.shape, lambda b, i: (0,) * a.ndim)
    return pl.pallas_call(
        functools.partial(_outproj_kernel, tc=tc),
        grid=(batch, nt),
        in_specs=[row(D_MODEL), row(DN_WIDTH), row(FX_WIDTH), row(POOL_WIDTH),
                  full(w_out), full(pool_bd), full(pool_scale)],
        out_specs=row(D_MODEL),
        out_shape=jax.ShapeDtypeStruct((n, D_MODEL), F32),
        scratch_shapes=[pltpu.VMEM((tc + POOL_WINDOWS[-1], POOL_WIDTH), F32)],
        compiler_params=pltpu.CompilerParams(dimension_semantics=("arbitrary", "arbitrary"),
                                             vmem_limit_bytes=VMEM_LIMIT),
        name="outproj",
    )(xf, o_dn, o_fx, pool_x, w_out, pool_bd, pool_scale)


def _ffn_kernel(x_ref, g_ref, wg_ref, wu_ref, wd_ref, o_ref, h_ref):
    f = pl.program_id(1)

    @pl.when(f == 0)
    def _():
        x = x_ref[...]
        ms = jnp.mean(x * x, axis=-1, keepdims=True)
        h_ref[...] = (x * lax.rsqrt(ms + EPS) * g_ref[...]).astype(BF16)
        o_ref[...] = x

    h = h_ref[...]
    a = jnp.dot(h, wg_ref[...], preferred_element_type=F32)
    u = jnp.dot(h, wu_ref[...], preferred_element_type=F32)
    t = (a * _sigmoid(a) * u).astype(BF16)
    o_ref[...] += jnp.dot(t, wd_ref[...], preferred_element_type=F32)


def _ffn(xf, g, w_gate, w_up, w_down, tm, tf):
    n = xf.shape[0]
    nf = D_FF // tf
    return pl.pallas_call(
        _ffn_kernel,
        grid=(n // tm, nf),
        in_specs=[
            pl.BlockSpec((tm, D_MODEL), lambda i, f: (i, 0)),
            pl.BlockSpec((1, D_MODEL), lambda i, f: (0, 0)),
            pl.BlockSpec((D_MODEL, tf), lambda i, f: (0, f)),
            pl.BlockSpec((D_MODEL, tf), lambda i, f: (0, f)),
            pl.BlockSpec((tf, D_MODEL), lambda i, f: (f, 0)),
        ],
        out_specs=pl.BlockSpec((tm, D_MODEL), lambda i, f: (i, 0)),
        out_shape=jax.ShapeDtypeStruct((n, D_MODEL), F32),
        scratch_shapes=[pltpu.VMEM((tm, D_MODEL), BF16)],
        compiler_params=pltpu.CompilerParams(dimension_semantics=("arbitrary", "arbitrary"),
                                             vmem_limit_bytes=VMEM_LIMIT),
        name="ffn",
    )(xf, g, w_gate, w_up, w_down)


MOE_SUB = 128
MOE_CHUNK = 256


def _moe_subtiles(tb):
    assert tb % MOE_CHUNK == 0 and MOE_CHUNK % MOE_SUB == 0
    return [(r, MOE_SUB) for r in range(0, tb, MOE_SUB)]


def _moe_kernel(x_ref, g_ref, rhi_ref, rlo_ref, wg_ref, wu_ref, wd_ref, o_ref,
                h_ref, gate_ref, rank_ref, rankt_ref, cnt_ref, y_ref):
    e = pl.program_id(1)
    tb = x_ref.shape[0]
    lane = lax.broadcasted_iota(jnp.int32, (tb, LANES), 1)

    @pl.when(e == 0)
    def _():
        x = x_ref[...]
        ms = jnp.mean(x * x, axis=-1, keepdims=True)
        hf = x * lax.rsqrt(ms + EPS) * g_ref[...]
        h_hi = hf.astype(BF16)
        h_lo = (hf - h_hi.astype(F32)).astype(BF16)
        h_ref[...] = h_hi
        o_ref[...] = x
        logits = (jnp.dot(h_hi, rhi_ref[...], preferred_element_type=F32)
                  + jnp.dot(h_lo, rhi_ref[...], preferred_element_type=F32)
                  + jnp.dot(h_hi, rlo_ref[...], preferred_element_type=F32))
        logits = jnp.where(lane < N_EXPERTS, logits, NEG_BIG)
        ex = jnp.exp(logits - jnp.max(logits, axis=-1, keepdims=True))
        probs = ex / jnp.sum(ex, axis=-1, keepdims=True)
        p1 = jnp.max(probs, axis=-1, keepdims=True)
        i1 = jnp.min(jnp.where(probs == p1, lane, LANES), axis=-1, keepdims=True)
        rest = jnp.where(lane == i1, -1.0, probs)
        p2 = jnp.max(rest, axis=-1, keepdims=True)
        i2 = jnp.min(jnp.where(rest == p2, lane, LANES), axis=-1, keepdims=True)
        denom = p1 + p2
        gate_ref[...] = (jnp.where(lane == i1, p1 / denom, 0.0)
                         + jnp.where(lane == i2, p2 / denom, 0.0))
        sel = jnp.where(jnp.logical_or(lane == i1, lane == i2), 1.0, 0.0)
        cum = _cumsum_rows(sel, tb)
        rank = jnp.where(sel > 0.0, cum - sel, -1.0)
        rank_ref[...] = rank
        rankt_ref[...] = rank.T
        cnt_ref[...] = cum[tb - 1:tb, :]

    cnt = jnp.sum(jnp.where(lane[0:1, :] == e, cnt_ref[...], 0.0)).astype(jnp.int32)
    sub_i = lax.broadcasted_iota(jnp.int32, (LANES, tb), 0)
    rank_row = jnp.sum(jnp.where(sub_i == e, rankt_ref[...], 0.0), axis=0, keepdims=True)
    rank_col = jnp.sum(jnp.where(lane == e, rank_ref[...], 0.0), axis=-1, keepdims=True)
    gate_col = jnp.sum(jnp.where(lane == e, gate_ref[...], 0.0), axis=-1, keepdims=True)

    subs = _moe_subtiles(tb)
    for c in range(tb // MOE_CHUNK):
        for r0, n_rows in [sub for sub in subs if sub[0] // MOE_CHUNK == c]:

            @pl.when(r0 < cnt)
            def _(r0=r0, n_rows=n_rows):
                rows = (lax.broadcasted_iota(jnp.int32, (n_rows, tb), 0) + r0).astype(F32)
                onehot = jnp.where(rank_row == rows, 1.0, 0.0).astype(BF16)
                xe = jnp.dot(onehot, h_ref[...], preferred_element_type=F32).astype(BF16)
                a = jnp.dot(xe, wg_ref[...], preferred_element_type=F32)
                u = jnp.dot(xe, wu_ref[...], preferred_element_type=F32)
                t = (a * _sigmoid(a) * u).astype(BF16)
                y_ref[r0:r0 + n_rows, :] = jnp.dot(t, wd_ref[...], preferred_element_type=F32).astype(BF16)

            if r0 % MOE_CHUNK != 0:
                @pl.when(jnp.logical_and(c * MOE_CHUNK < cnt, r0 >= cnt))
                def _(r0=r0, n_rows=n_rows):
                    y_ref[r0:r0 + n_rows, :] = jnp.zeros((n_rows, D_MODEL), BF16)

        @pl.when(c * MOE_CHUNK < cnt)
        def _():
            cols = (lax.broadcasted_iota(jnp.int32, (tb, MOE_CHUNK), 1) + c * MOE_CHUNK).astype(F32)
            scatter = jnp.where(rank_col == cols, gate_col, 0.0).astype(BF16)
            o_ref[...] += jnp.dot(scatter, y_ref[c * MOE_CHUNK:(c + 1) * MOE_CHUNK, :],
                                  preferred_element_type=F32)


def _moe(xf, g, r_hi, r_lo, w_gate, w_up, w_down, tb):
    n = xf.shape[0]
    assert tb % MOE_CHUNK == 0
    return pl.pallas_call(
        _moe_kernel,
        grid=(n // tb, N_EXPERTS),
        in_specs=[
            pl.BlockSpec((tb, D_MODEL), lambda i, e: (i, 0)),
            pl.BlockSpec((1, D_MODEL), lambda i, e: (0, 0)),
            pl.BlockSpec((D_MODEL, LANES), lambda i, e: (0, 0)),
            pl.BlockSpec((D_MODEL, LANES), lambda i, e: (0, 0)),
            pl.BlockSpec((None, D_MODEL, D_FF_EXPERT), lambda i, e: (e, 0, 0)),
            pl.BlockSpec((None, D_MODEL, D_FF_EXPERT), lambda i, e: (e, 0, 0)),
            pl.BlockSpec((None, D_FF_EXPERT, D_MODEL), lambda i, e: (e, 0, 0)),
        ],
        out_specs=pl.BlockSpec((tb, D_MODEL), lambda i, e: (i, 0)),
        out_shape=jax.ShapeDtypeStruct((n, D_MODEL), F32),
        scratch_shapes=[
            pltpu.VMEM((tb, D_MODEL), BF16),
            pltpu.VMEM((tb, LANES), F32),
            pltpu.VMEM((tb, LANES), F32),
            pltpu.VMEM((LANES, tb), F32),
            pltpu.VMEM((1, LANES), F32),
            pltpu.VMEM((tb, D_MODEL), BF16),
        ],
        compiler_params=pltpu.CompilerParams(dimension_semantics=("arbitrary", "arbitrary"),
                                             vmem_limit_bytes=VMEM_LIMIT),
        name="moe",
    )(xf, g, r_hi, r_lo, w_gate, w_up, w_down)


def _pad_lanes(v, offset):
    return jnp.zeros((1, LANES), F32).at[0, offset:offset + v.shape[0]].set(v.astype(F32))


def _block_diag_ones(width, group):
    idx = np.arange(width) // group
    return jnp.asarray((idx[:, None] == idx[None, :]).astype(np.float32), BF16)


def _tile(n, pref):
    return pref if n % pref == 0 else n


def _mixer(xf, batch, seq, norm1, w_in, dn_conv, dn_a_log, dn_dt_bias, dn_onorm, fx_qnorm, fx_knorm,
           fx_f_bias, pool_w, pool_scale, w_out):
    n = batch * seq
    w_big, w_small = _winprep(w_in)

    dn_qkv, dn_z, fx_qkv, pool_x, small = _inproj(
        xf, norm1.reshape(1, D_MODEL), w_big, w_small, _tile(n, 1024))

    bd = _block_diag_ones(DN_WIDTH, HEAD_DIM)
    gpar = jnp.concatenate([_pad_lanes(dn_a_log, LANE_A), _pad_lanes(dn_dt_bias, LANE_A)], axis=0)
    o_dn = _deltanet(dn_qkv, dn_z, small, dn_conv, gpar, jnp.tile(dn_onorm, DN_HEADS).reshape(1, DN_WIDTH),
                     bd, batch, seq, _tile(seq, 256))

    qg = (jnp.tile(fx_qnorm, FX_HEADS) * (HEAD_DIM ** -0.5 * LOG2E)).reshape(1, FX_WIDTH)
    kg = jnp.tile(fx_knorm, FX_HEADS).reshape(1, FX_WIDTH)
    qa, ka = _fxprep(fx_qkv, small, _pad_lanes(fx_f_bias, LANE_F), qg, kg, bd, batch, seq, _tile(seq, 512))
    o_fx = _fxattn(qa, ka, fx_qkv, batch, seq, _tile(seq, 512), _tile(seq, 512), 2)

    pool_bd = jax.scipy.linalg.block_diag(*[pool_w[gi] for gi in range(POOL_GROUPS)]).astype(BF16)
    return _outproj(xf, o_dn, o_fx, pool_x, w_out.astype(BF16), pool_bd,
                    pool_scale.reshape(1, POOL_WIDTH), batch, seq, _tile(seq, 1024))


def kernel(x, norm1, w_in, dn_conv, dn_a_log, dn_dt_bias, dn_onorm, fx_qnorm, fx_knorm, fx_f_bias, pool_w,
           pool_scale, w_out, norm2, ffn_gate, ffn_up, ffn_down, router, moe_gate, moe_up, moe_down):
    batch, seq, _ = x.shape
    n = batch * seq
    xf = x.reshape(n, D_MODEL)
    depth = norm1.shape[0]
    for layer in range(depth):
        xf = _mixer(xf, batch, seq, norm1[layer], w_in[layer], dn_conv[layer], dn_a_log[layer],
                    dn_dt_bias[layer], dn_onorm[layer], fx_qnorm[layer], fx_knorm[layer],
                    fx_f_bias[layer], pool_w[layer], pool_scale[layer], w_out[layer])
        g2 = norm2[layer].reshape(1, D_MODEL)
        j = layer // 2
        if layer % 2 == 0:
            xf = _ffn(xf, g2, ffn_gate[j].astype(BF16), ffn_up[j].astype(BF16), ffn_down[j].astype(BF16),
                      _tile(n, 1024), D_FF // 2)
        else:
            r = jnp.concatenate([router[j], jnp.zeros((D_MODEL, LANES - N_EXPERTS), F32)], axis=1)
            r_hi = r.astype(BF16)
            r_lo = (r - r_hi.astype(F32)).astype(BF16)
            xf = _moe(xf, g2, r_hi, r_lo, moe_gate[j].astype(BF16), moe_up[j].astype(BF16),
                      moe_down[j].astype(BF16), _tile(n, 1024))
    return xf.reshape(batch, seq, D_MODEL)
```
